```python
import math
import jax, jax.numpy as jnp
from jax import lax
import numpy as np

D_MODEL = 4096
BATCH = 2
SEQ = 8192
DEPTH = 4

HEAD_DIM = 128
Q_BLOCK = 128
NEG_INF = -1e30

N_HEADS_SB = D_MODEL // (2 * HEAD_DIM)
N_HEADS_DSA = D_MODEL // (2 * HEAD_DIM)
N_IDX_HEADS = 8
IDX_DIM = 64
TOPK_MAX = 256
N_BUCKETS = 32
MAX_DISTANCE = 128
N_HEADS_MLA = D_MODEL // HEAD_DIM
Q_LORA = 1024
KV_LORA = 512
QK_NOPE = 128
QK_ROPE = 64
V_DIM = 128
ROPE_THETA = 10000.0
N_EXPERTS = 16
N_GROUPS = 4
EXPERTS_PER_GROUP = N_EXPERTS // N_GROUPS
TOP_K_EXPERTS = 2
D_FF_EXPERT = 768
DEEPNORM_ALPHA = (2 * DEPTH) ** 0.25
DEEPNORM_BETA = (8 * DEPTH) ** -0.25
LN_EPS = 1e-5
RMS_EPS = 1e-6

N_EVEN = (DEPTH + 1) // 2
N_ODD = DEPTH // 2
SB_W = N_HEADS_SB * HEAD_DIM
DSA_W = N_HEADS_DSA * HEAD_DIM
EVEN_SPLITS = (SB_W, SB_W, SB_W, DSA_W, HEAD_DIM, HEAD_DIM, N_IDX_HEADS * IDX_DIM, N_IDX_HEADS, IDX_DIM)
EVEN_IN = sum(EVEN_SPLITS)
MLA_SPLITS = (Q_LORA, KV_LORA, QK_ROPE)
MLA_DOWN = sum(MLA_SPLITS)

kernel_name = 'hybrid_stickbreak_dsa_mla_grouped_moe_deepnorm'


def split_last(t, sizes):
    return jnp.split(t, np.cumsum(sizes)[:-1].tolist(), axis=-1)


def layer_norm(x, g, b):
    xf = x.astype(jnp.float32)
    mu = jnp.mean(xf, axis=-1, keepdims=True)
    var = jnp.mean(jnp.square(xf - mu), axis=-1, keepdims=True)
    y = (xf - mu) * lax.rsqrt(var + LN_EPS) * g.astype(jnp.float32) + b.astype(jnp.float32)
    return y.astype(x.dtype)


def rms_norm(x, g):
    xf = x.astype(jnp.float32)
    y = xf * lax.rsqrt(jnp.mean(jnp.square(xf), axis=-1, keepdims=True) + RMS_EPS) * g.astype(jnp.float32)
    return y.astype(x.dtype)


def rope(x, pos):
    half = x.shape[-1] // 2
    inv_freq = ROPE_THETA ** (-jnp.arange(half, dtype=jnp.float32) / half)
    ang = pos.astype(jnp.float32)[..., None] * inv_freq
    ang = ang.reshape(ang.shape[:2] + (1,) * (x.ndim - 3) + (half,))
    cos, sin = jnp.cos(ang), jnp.sin(ang)
    xf = x.astype(jnp.float32)
    x1, x2 = xf[..., :half], xf[..., half:]
    return jnp.concatenate([x1 * cos - x2 * sin, x1 * sin + x2 * cos], axis=-1).astype(x.dtype)


def t5_bucket(dist):
    n = jnp.maximum(dist, 0)
    max_exact = N_BUCKETS // 2
    scaled = jnp.log(jnp.maximum(n, 1).astype(jnp.float32) / max_exact) / math.log(MAX_DISTANCE / max_exact)
    large = jnp.minimum(max_exact + (scaled * (N_BUCKETS - max_exact)).astype(jnp.int32), N_BUCKETS - 1)
    return jnp.where(n < max_exact, n, large)


def to_blocks(t):
    b, s = t.shape[:2]
    return jnp.moveaxis(t.reshape((b, s // Q_BLOCK, Q_BLOCK) + t.shape[2:]), 1, 0)


def from_blocks(t):
    nb, b, qb = t.shape[:3]
    return jnp.moveaxis(t, 0, 1).reshape((b, nb * qb) + t.shape[3:])


def block_starts(s):
    return jnp.arange(s // Q_BLOCK, dtype=jnp.int32) * Q_BLOCK


def stick_breaking_attention(q, k, v):
    s = q.shape[1]
    key_pos = jnp.arange(s)

    def block(args):
        qb, start = args
        t = start + jnp.arange(Q_BLOCK)
        strict = key_pos[None, :] < t[:, None]
        z = jnp.einsum('bqhd,bkhd->bhqk', qb, k, preferred_element_type=jnp.float32) * HEAD_DIM ** -0.5
        log_keep = jnp.where(strict, jax.nn.log_sigmoid(-z), 0.0)
        after = lax.cumsum(log_keep, axis=3, reverse=True) - log_keep
        weight = jnp.where(strict, jnp.exp(jax.nn.log_sigmoid(z) + after), 0.0)
        return jnp.einsum('bhqk,bkhd->bqhd', weight.astype(v.dtype), v)

    return from_blocks(lax.map(block, (to_blocks(q), block_starts(s))))


def dsa_attention(q, k, v, q_idx, w_idx, k_idx, pos, rel_bias):
    s = q.shape[1]
    topk = min(TOPK_MAX, s // 4)
    key_pos = jnp.arange(s)
    take = jax.vmap(lambda a, i: a[i])

    def block(args):
        qb, qib, wb, pb, start = args
        t = start + jnp.arange(Q_BLOCK)
        causal = key_pos[None, :] <= t[:, None]
        idx_logits = jnp.einsum('bqhi,bki->bqhk', qib, k_idx, preferred_element_type=jnp.float32) * IDX_DIM ** -0.5
        score = jnp.einsum('bqh,bqhk->bqk', wb.astype(jnp.float32), jax.nn.relu(idx_logits)) * N_IDX_HEADS ** -0.5
        score = jnp.where(causal[None], score, NEG_INF)
        _, sel = lax.top_k(score, topk)
        valid = sel <= t[None, :, None]
        k_sel = take(k, sel)
        v_sel = take(v, sel)
        bias = rel_bias[t5_bucket(pb[:, :, None] - take(pos, sel))]
        logits = jnp.einsum('bqhd,bqkd->bhqk', qb, k_sel, preferred_element_type=jnp.float32) * HEAD_DIM ** -0.5
        logits = logits + jnp.transpose(bias, (0, 3, 1, 2)).astype(jnp.float32)
        logits = jnp.where(valid[:, None], logits, NEG_INF)
        p = jax.nn.softmax(logits, axis=-1)
        return jnp.einsum('bhqk,bqkd->bqhd', p.astype(v.dtype), v_sel)

    xs = (to_blocks(q), to_blocks(q_idx), to_blocks(w_idx), to_blocks(pos), block_starts(s))
    return from_blocks(lax.map(block, xs))


def even_mixer(x, pos, w_in, w_out, rel_bias):
    b, s, _ = x.shape
    q_sb, k_sb, v_sb, q_ds, k_ds, v_ds, q_ix, w_ix, k_ix = split_last(x @ w_in, EVEN_SPLITS)
    heads = lambda t, h, d: t.reshape(b, s, h, d)
    o_sb = stick_breaking_attention(heads(q_sb, N_HEADS_SB, HEAD_DIM), heads(k_sb, N_HEADS_SB, HEAD_DIM),
                                    heads(v_sb, N_HEADS_SB, HEAD_DIM))
    o_ds = dsa_attention(heads(q_ds, N_HEADS_DSA, HEAD_DIM), k_ds, v_ds, heads(q_ix, N_IDX_HEADS, IDX_DIM),
                         w_ix, k_ix, pos, rel_bias)
    o = jnp.concatenate([o_sb.reshape(b, s, SB_W), o_ds.reshape(b, s, DSA_W)], axis=-1)
    return o @ w_out


def mla_attention(x, pos, w_down, g_q, g_kv, w_uq, w_ukv, w_o):
    b, s, _ = x.shape
    c_q, c_kv, k_rope = split_last(x @ w_down, MLA_SPLITS)
    q = (rms_norm(c_q, g_q) @ w_uq).reshape(b, s, N_HEADS_MLA, QK_NOPE + QK_ROPE)
    kv = (rms_norm(c_kv, g_kv) @ w_ukv).reshape(b, s, N_HEADS_MLA, QK_NOPE + V_DIM)
    q_nope, q_rope = q[..., :QK_NOPE], rope(q[..., QK_NOPE:], pos)
    k_nope, v = kv[..., :QK_NOPE], kv[..., QK_NOPE:]
    k_rope = rope(k_rope, pos)
    scale = (QK_NOPE + QK_ROPE) ** -0.5
    key_pos = jnp.arange(s)

    def block(args):
        qn, qr, start = args
        t = start + jnp.arange(Q_BLOCK)
        causal = key_pos[None, :] <= t[:, None]
        logits = (jnp.einsum('bqhd,bkhd->bhqk', qn, k_nope, preferred_element_type=jnp.float32)
                  + jnp.einsum('bqhr,bkr->bhqk', qr, k_rope, preferred_element_type=jnp.float32)) * scale
        logits = jnp.where(causal, logits, NEG_INF)
        p = jax.nn.softmax(logits, axis=-1)
        return jnp.einsum('bhqk,bkhd->bqhd', p.astype(v.dtype), v)

    out = from_blocks(lax.map(block, (to_blocks(q_nope), to_blocks(q_rope), block_starts(s))))
    return out.reshape(b, s, N_HEADS_MLA * V_DIM) @ w_o


def moe(x, w_router, router_bias, w_gate, w_up, w_down):
    b, s, d = x.shape
    xt = x.reshape(b * s, d)
    affinity = jax.nn.sigmoid(jnp.dot(xt, w_router, preferred_element_type=jnp.float32))
    biased = affinity + router_bias.astype(jnp.float32)
    group_score = jnp.sum(lax.top_k(biased.reshape(-1, N_GROUPS, EXPERTS_PER_GROUP), TOP_K_EXPERTS)[0], axis=-1)
    group = jnp.argmax(group_score, axis=-1)
    in_group = (jnp.arange(N_EXPERTS) // EXPERTS_PER_GROUP)[None, :] == group[:, None]
    _, top_idx = lax.top_k(jnp.where(in_group, biased, NEG_INF), TOP_K_EXPERTS)
    top_w = jnp.take_along_axis(affinity, top_idx, axis=-1)
    top_w = top_w / jnp.sum(top_w, axis=-1, keepdims=True)
    gates = jnp.sum(jax.nn.one_hot(top_idx, N_EXPERTS, dtype=jnp.float32) * top_w[..., None], axis=1).astype(x.dtype)
    y = jnp.zeros_like(xt)
    for e in range(N_EXPERTS):
        h = jax.nn.silu(xt @ w_gate[e]) * (xt @ w_up[e])
        y = y + gates[:, e:e + 1] * (h @ w_down[e])
    return y.reshape(b, s, d)


def setup_inputs(seed: int = 0) -> dict:
    key = jax.random.key(seed)
    ks = jax.random.split(key, 18)
    f32 = jnp.float32

    def normal(k, shape, scale):
        return jax.random.normal(k, shape, f32) * scale

    x = normal(ks[0], (BATCH, SEQ, D_MODEL), 1.0)
    offsets = jax.random.randint(ks[1], (BATCH, 1), 0, 1024, dtype=jnp.int32)
    positions = offsets + jnp.arange(SEQ, dtype=jnp.int32)[None, :]
    rel_bias = normal(ks[2], (N_BUCKETS, N_HEADS_DSA), 0.5)
    even_w_in = normal(ks[3], (N_EVEN, D_MODEL, EVEN_IN), D_MODEL ** -0.5)
    even_w_out = normal(ks[4], (N_EVEN, SB_W + DSA_W, D_MODEL), (SB_W + DSA_W) ** -0.5 * DEEPNORM_BETA)
    mla_w_down = normal(ks[5], (N_ODD, D_MODEL, MLA_DOWN), D_MODEL ** -0.5)
    mla_g_q = 1.0 + normal(ks[6], (N_ODD, Q_LORA), 0.02)
    mla_g_kv = 1.0 + normal(ks[7], (N_ODD, KV_LORA), 0.02)
    mla_w_uq = normal(ks[8], (N_ODD, Q_LORA, N_HEADS_MLA * (QK_NOPE + QK_ROPE)), Q_LORA ** -0.5)
    mla_w_ukv = normal(ks[9], (N_ODD, KV_LORA, N_HEADS_MLA * (QK_NOPE + V_DIM)), KV_LORA ** -0.5)
    mla_w_o = normal(ks[10], (N_ODD, N_HEADS_MLA * V_DIM, D_MODEL), (N_HEADS_MLA * V_DIM) ** -0.5 * DEEPNORM_BETA)
    w_router = normal(ks[11], (D_MODEL, N_EXPERTS), D_MODEL ** -0.5)
    router_bias = normal(ks[12], (N_EXPERTS,), 0.01)
    exp_w_gate = normal(ks[13], (DEPTH, N_EXPERTS, D_MODEL, D_FF_EXPERT), D_MODEL ** -0.5)
    exp_w_up = normal(ks[14], (DEPTH, N_EXPERTS, D_MODEL, D_FF_EXPERT), D_MODEL ** -0.5)
    exp_w_down = normal(ks[15], (DEPTH, N_EXPERTS, D_FF_EXPERT, D_MODEL), D_FF_EXPERT ** -0.5 * DEEPNORM_BETA)
    ln_g = 1.0 + normal(ks[16], (DEPTH, 2, D_MODEL), 0.02)
    ln_b = normal(ks[17], (DEPTH, 2, D_MODEL), 0.02)
    return {'x': x, 'positions': positions, 'rel_bias': rel_bias,
            'even_w_in': even_w_in, 'even_w_out': even_w_out,
            'mla_w_down': mla_w_down, 'mla_g_q': mla_g_q, 'mla_g_kv': mla_g_kv,
            'mla_w_uq': mla_w_uq, 'mla_w_ukv': mla_w_ukv, 'mla_w_o': mla_w_o,
            'w_router': w_router, 'router_bias': router_bias,
            'exp_w_gate': exp_w_gate, 'exp_w_up': exp_w_up, 'exp_w_down': exp_w_down,
            'ln_g': ln_g, 'ln_b': ln_b}


def reference(x, positions, rel_bias, even_w_in, even_w_out, mla_w_down, mla_g_q, mla_g_kv,
              mla_w_uq, mla_w_ukv, mla_w_o, w_router, router_bias, exp_w_gate, exp_w_up,
              exp_w_down, ln_g, ln_b):
    for layer in range(DEPTH):
        i = layer // 2
        if layer % 2 == 0:
            mix = even_mixer(x, positions, even_w_in[i], even_w_out[i], rel_bias)
        else:
            mix = mla_attention(x, positions, mla_w_down[i], mla_g_q[i], mla_g_kv[i],
                                mla_w_uq[i], mla_w_ukv[i], mla_w_o[i])
        x = layer_norm(DEEPNORM_ALPHA * x + mix, ln_g[layer, 0], ln_b[layer, 0])
        ffn = moe(x, w_router, router_bias, exp_w_gate[layer], exp_w_up[layer], exp_w_down[layer])
        x = layer_norm(DEEPNORM_ALPHA * x + ffn, ln_g[layer, 1], ln_b[layer, 1])
    return x
```

```python
import functools
import math

import jax
import jax.numpy as jnp
from jax import lax
from jax.experimental import pallas as pl
from jax.experimental.pallas import tpu as pltpu

F32 = jnp.float32
BF16 = jnp.bfloat16
I32 = jnp.int32

D_MODEL = 4096
DEPTH = 4
HEAD_DIM = 128
N_HEADS_SB = 16
N_HEADS_DSA = 16
N_IDX_HEADS = 8
IDX_DIM = 64
TOPK_MAX = 256
N_BUCKETS = 32
MAX_DISTANCE = 128
N_HEADS_MLA = 32
Q_LORA = 1024
KV_LORA = 512
QK_NOPE = 128
QK_ROPE = 64
V_DIM = 128
ROPE_THETA = 10000.0
N_EXPERTS = 16
N_GROUPS = 4
EXPERTS_PER_GROUP = N_EXPERTS // N_GROUPS
D_FF_EXPERT = 768
DEEPNORM_ALPHA = (2 * DEPTH) ** 0.25
LN_EPS = 1e-5
RMS_EPS = 1e-6
NEG_INF = -1e30
SB_W = N_HEADS_SB * HEAD_DIM
DSA_W = N_HEADS_DSA * HEAD_DIM

LANES = 128
INT_MIN = -(2 ** 31)
VMEM_LIMIT = 56 * 1024 * 1024

_NT = (((1,), (1,)), ((), ()))


def _params(sem, vmem=VMEM_LIMIT):
    return pltpu.CompilerParams(dimension_semantics=sem, vmem_limit_bytes=vmem)


def _mm_kernel(a_ref, b_ref, o_ref):
    o_ref[...] = jnp.dot(a_ref[...], b_ref[...], preferred_element_type=F32).astype(o_ref.dtype)


def _mm2_kernel(a1_ref, a2_ref, b1_ref, b2_ref, o_ref):
    acc = jnp.dot(a1_ref[...], b1_ref[...], preferred_element_type=F32)
    acc = acc + jnp.dot(a2_ref[...], b2_ref[...], preferred_element_type=F32)
    o_ref[...] = acc.astype(o_ref.dtype)


def matmul(a, b, out_dtype, tm, tn):
    m, k = a.shape
    n = b.shape[1]
    tm, tn = min(tm, m), min(tn, n)
    assert m % tm == 0 and n % tn == 0
    return pl.pallas_call(
        _mm_kernel,
        grid=(m // tm, n // tn),
        in_specs=[pl.BlockSpec((tm, k), lambda i, j: (i, 0)),
                  pl.BlockSpec((k, tn), lambda i, j: (0, j))],
        out_specs=pl.BlockSpec((tm, tn), lambda i, j: (i, j)),
        out_shape=jax.ShapeDtypeStruct((m, n), out_dtype),
        compiler_params=_params(("parallel", "arbitrary")),
        name="matmul",
    )(a, b)


def matmul2(a1, a2, b1, b2, out_dtype, tm, tn):
    m, k1 = a1.shape
    k2 = a2.shape[1]
    n = b1.shape[1]
    tm, tn = min(tm, m), min(tn, n)
    assert m % tm == 0 and n % tn == 0
    return pl.pallas_call(
        _mm2_kernel,
        grid=(m // tm, n // tn),
        in_specs=[pl.BlockSpec((tm, k1), lambda i, j: (i, 0)),
                  pl.BlockSpec((tm, k2), lambda i, j: (i, 0)),
                  pl.BlockSpec((k1, tn), lambda i, j: (0, j)),
                  pl.BlockSpec((k2, tn), lambda i, j: (0, j))],
        out_specs=pl.BlockSpec((tm, tn), lambda i, j: (i, j)),
        out_shape=jax.ShapeDtypeStruct((m, n), out_dtype),
        compiler_params=_params(("parallel", "arbitrary")),
        name="matmul2",
    )(a1, a2, b1, b2)


def _ln_kernel(x_ref, m_ref, g_ref, b_ref, o_ref, ob_ref):
    h = DEEPNORM_ALPHA * x_ref[...] + m_ref[...]
    mu = jnp.mean(h, axis=-1, keepdims=True)
    d = h - mu
    var = jnp.mean(d * d, axis=-1, keepdims=True)
    y = d * lax.rsqrt(var + LN_EPS) * g_ref[...] + b_ref[...]
    o_ref[...] = y
    ob_ref[...] = y.astype(BF16)


def deepnorm_ln(x, mix, g, b, tm=128):
    t, d = x.shape
    tm = min(tm, t)
    row = pl.BlockSpec((tm, d), lambda i: (i, 0))
    vec = pl.BlockSpec((1, d), lambda i: (0, 0))
    return pl.pallas_call(
        _ln_kernel,
        grid=(t // tm,),
        in_specs=[row, row, vec, vec],
        out_specs=[row, row],
        out_shape=[jax.ShapeDtypeStruct((t, d), F32), jax.ShapeDtypeStruct((t, d), BF16)],
        compiler_params=_params(("parallel",)),
        name="deepnorm_ln",
    )(x, mix, g.reshape(1, d), b.reshape(1, d))


def _rms_kernel(x_ref, g_ref, o_ref):
    x = x_ref[...]
    y = x * lax.rsqrt(jnp.mean(x * x, axis=-1, keepdims=True) + RMS_EPS) * g_ref[...]
    o_ref[...] = y.astype(o_ref.dtype)


def rms_norm(x, g, tm=256):
    t, d = x.shape
    tm = min(tm, t)
    return pl.pallas_call(
        _rms_kernel,
        grid=(t // tm,),
        in_specs=[pl.BlockSpec((tm, d), lambda i: (i, 0)), pl.BlockSpec((1, d), lambda i: (0, 0))],
        out_specs=pl.BlockSpec((tm, d), lambda i: (i, 0)),
        out_shape=jax.ShapeDtypeStruct((t, d), BF16),
        compiler_params=_params(("parallel",)),
        name="rms_norm",
    )(x, g.reshape(1, d))


def _rope_kernel(x_ref, pos_ref, f_ref, o_ref):
    lane = lax.broadcasted_iota(I32, (1, LANES), 1)
    first = (lane % QK_ROPE) < (QK_ROPE // 2)
    ang = pos_ref[...].astype(F32) * f_ref[...]
    cos = jnp.cos(ang)
    sin = jnp.sin(ang)
    sin = jnp.where(first, -sin, sin)
    for c in range(x_ref.shape[1] // LANES):
        x = x_ref[:, c * LANES:(c + 1) * LANES]
        partner = jnp.where(first, pltpu.roll(x, LANES - QK_ROPE // 2, 1), pltpu.roll(x, QK_ROPE // 2, 1))
        o_ref[:, c * LANES:(c + 1) * LANES] = (x * cos + partner * sin).astype(o_ref.dtype)


def rope(x, pos_col, inv_freq_lanes, tm=256):
    t, w = x.shape
    tm = min(tm, t)
    return pl.pallas_call(
        _rope_kernel,
        grid=(t // tm,),
        in_specs=[pl.BlockSpec((tm, w), lambda i: (i, 0)),
                  pl.BlockSpec((tm, 1), lambda i: (i, 0)),
                  pl.BlockSpec((1, LANES), lambda i: (0, 0))],
        out_specs=pl.BlockSpec((tm, w), lambda i: (i, 0)),
        out_shape=jax.ShapeDtypeStruct((t, w), BF16),
        compiler_params=_params(("parallel",)),
        name="rope",
    )(x, pos_col, inv_freq_lanes)


def _sb_kernel(q_ref, k_ref, v_ref, o_ref, *, tile):
    i = pl.program_id(2)
    q = q_ref[...]
    row = lax.broadcasted_iota(I32, (tile, tile), 0)
    col = lax.broadcasted_iota(I32, (tile, tile), 1)
    later = (row > col).astype(BF16)
    strict = col < row

    def step(j, carry, acc, diag):
        sl = pl.ds(pl.multiple_of(j * tile, tile), tile)
        z = lax.dot_general(q, k_ref[sl, :], _NT, preferred_element_type=F32)
        t = jnp.log(1.0 + jnp.exp(-jnp.abs(z)))
        log_keep = -(jnp.maximum(z, 0.0) + t)
        if diag:
            log_keep = jnp.where(strict, log_keep, 0.0)
        hi = log_keep.astype(BF16)
        lo = (log_keep - hi.astype(F32)).astype(BF16)
        after = (jnp.dot(hi, later, preferred_element_type=F32)
                 + jnp.dot(lo, later, preferred_element_type=F32))
        w = jnp.exp(jnp.minimum(z, 0.0) - t + after + carry)
        if diag:
            w = jnp.where(strict, w, 0.0)
        acc = acc + jnp.dot(w.astype(BF16), v_ref[sl, :], preferred_element_type=F32)
        carry = carry + jnp.sum(log_keep, axis=1, keepdims=True)
        return carry, acc

    carry, acc = step(i, jnp.zeros((tile, 1), F32), jnp.zeros((tile, HEAD_DIM), F32), True)
    carry, acc = lax.fori_loop(0, i, lambda n, c: step(i - 1 - n, c[0], c[1], False), (carry, acc))
    o_ref[...] = acc.astype(o_ref.dtype)


def sb_attention(qkv, batch, tile=256):
    t = qkv.shape[0]
    s = t // batch
    h = qkv.shape[1] // (3 * HEAD_DIM)
    tile = min(tile, s)
    nq = s // tile
    return pl.pallas_call(
        functools.partial(_sb_kernel, tile=tile),
        grid=(batch, h, nq),
        in_specs=[pl.BlockSpec((tile, HEAD_DIM), lambda b, hh, i: (b * nq + i, hh)),
                  pl.BlockSpec((s, HEAD_DIM), lambda b, hh, i: (b, h + hh)),
                  pl.BlockSpec((s, HEAD_DIM), lambda b, hh, i: (b, 2 * h + hh))],
        out_specs=pl.BlockSpec((tile, HEAD_DIM), lambda b, hh, i: (b * nq + i, hh)),
        out_shape=jax.ShapeDtypeStruct((t, h * HEAD_DIM), BF16),
        compiler_params=_params(("parallel", "parallel", "arbitrary")),
        name="sb_attention",
    )(qkv, qkv, qkv)


def _t5_bucket(dist):
    n = jnp.maximum(dist, 0)
    max_exact = N_BUCKETS // 2
    scaled = jnp.log(jnp.maximum(n, 1).astype(F32) / max_exact) / math.log(MAX_DISTANCE / max_exact)
    large = jnp.minimum(max_exact + (scaled * (N_BUCKETS - max_exact)).astype(I32), N_BUCKETS - 1)
    return jnp.where(n < max_exact, n, large)


def _dsa_kernel(far_ref, rb_ref,
                q_ref, k_ref, v_ref, qi_ref, wk_q_ref, wk_k_ref, posq_ref, posk_ref,
                o_ref,
                keys_ref, qall_ref, s_ref, p_ref, m_ref, l_ref, acc_ref,
                *, tq, tk, topk, n_heads):
    b = pl.program_id(0)
    i = pl.program_id(1)
    nq = pl.num_programs(1)
    n_tiles = ((i + 1) * tq + tk - 1) // tk
    t_idx = i * tq + lax.broadcasted_iota(I32, (tq, tk), 0)
    col = lax.broadcasted_iota(I32, (tq, tk), 1)

    qi = qi_ref[...].astype(BF16)
    w_idx = wk_q_ref[:, IDX_DIM:IDX_DIM + N_IDX_HEADS]

    def score_tile(j, _):
        sl = pl.ds(pl.multiple_of(j * tk, tk), tk)
        kj = wk_k_ref[sl, 0:IDX_DIM].astype(BF16)
        sc = jnp.zeros((tq, tk), F32)
        for hh in range(N_IDX_HEADS):
            lg = lax.dot_general(qi[:, hh * IDX_DIM:(hh + 1) * IDX_DIM], kj, _NT,
                                 preferred_element_type=F32)
            sc = sc + w_idx[:, hh:hh + 1] * jnp.maximum(lg, 0.0)
        sc = sc * (N_IDX_HEADS ** -0.5)
        sc = jnp.where(j * tk + col <= t_idx, sc, NEG_INF)
        sc = jnp.where(sc == 0.0, 0.0, sc)
        bits = pltpu.bitcast(sc, I32)
        keys_ref[:, sl] = bits ^ ((bits >> 31) & 0x7FFFFFFF)
        return 0

    lax.fori_loop(0, n_tiles, score_tile, 0)

    def count(pred):
        def body(j, acc):
            sl = pl.ds(pl.multiple_of(j * tk, tk), tk)
            return acc + jnp.where(pred(keys_ref[:, sl]), 1, 0)
        acc = lax.fori_loop(0, n_tiles, body, jnp.zeros((tq, tk), I32))
        return jnp.sum(acc, axis=1, keepdims=True)

    def bisect(it, off):
        trial = off | jnp.left_shift(jnp.int32(1), 31 - it)
        cand = trial + jnp.int32(INT_MIN)
        return jnp.where(count(lambda kk: kk >= cand) >= topk, trial, off)

    thr = lax.fori_loop(0, 32, bisect, jnp.zeros((tq, 1), I32)) + jnp.int32(INT_MIN)
    need = (topk - count(lambda kk: kk > thr)).astype(F32)

    for hh in range(n_heads):
        qall_ref[hh * tq:(hh + 1) * tq, :] = q_ref[:, hh * HEAD_DIM:(hh + 1) * HEAD_DIM]
    m_ref[...] = jnp.full(m_ref.shape, NEG_INF, F32)
    l_ref[...] = jnp.zeros(l_ref.shape, F32)
    acc_ref[...] = jnp.zeros(acc_ref.shape, F32)
    r2 = lax.broadcasted_iota(I32, (tk, tk), 0)
    c2 = lax.broadcasted_iota(I32, (tk, tk), 1)
    earlier = (r2 < c2).astype(BF16)
    pos_q = posq_ref[...]

    def attend_tile(j, eq_before):
        sl = pl.ds(pl.multiple_of(j * tk, tk), tk)
        kt = keys_ref[:, sl]
        eq = jnp.where(kt == thr, 1.0, 0.0)
        rank = eq_before + jnp.dot(eq.astype(BF16), earlier, preferred_element_type=F32)
        take = jnp.where(kt > thr, 1.0, jnp.where(rank < need, eq, 0.0))
        mask = jnp.where(j * tk + col <= t_idx, take, 0.0) > 0.5
        s_ref[...] = lax.dot_general(qall_ref[...], k_ref[sl, :], _NT, preferred_element_type=F32)
        is_far = far_ref[(b * nq + i) * (nq * tq // tk) + j] == 1
        bucket = _t5_bucket(pos_q - posk_ref[0, :, sl])

        def head(hh, _):
            rows = pl.ds(pl.multiple_of(hh * tq, tq), tq)

            def far_bias():
                return jnp.full((tq, tk), rb_ref[(N_BUCKETS - 1) * n_heads + hh], F32)

            def near_bias():
                bias = jnp.zeros((tq, tk), F32)
                for bb in range(N_BUCKETS):
                    bias = jnp.where(bucket == bb, rb_ref[bb * n_heads + hh], bias)
                return bias

            sc = s_ref[rows, :] + lax.cond(is_far, far_bias, near_bias)
            sc = jnp.where(mask, sc, NEG_INF)
            m_old = m_ref[rows, :]
            m_new = jnp.maximum(m_old, jnp.max(sc, axis=1, keepdims=True))
            alpha = jnp.exp(m_old - m_new)
            p = jnp.exp(sc - m_new)
            l_ref[rows, :] = alpha * l_ref[rows, :] + jnp.sum(p, axis=1, keepdims=True)
            m_ref[rows, :] = m_new
            acc_ref[rows, :] = alpha * acc_ref[rows, :]
            p_ref[rows, :] = p.astype(BF16)
            return 0

        lax.fori_loop(0, n_heads, head, 0)
        acc_ref[...] += jnp.dot(p_ref[...], v_ref[sl, :], preferred_element_type=F32)
        return eq_before + jnp.sum(eq, axis=1, keepdims=True)

    lax.fori_loop(0, n_tiles, attend_tile, jnp.zeros((tq, 1), F32))
    for hh in range(n_heads):
        rows = slice(hh * tq, (hh + 1) * tq)
        o_ref[:, hh * HEAD_DIM:(hh + 1) * HEAD_DIM] = (acc_ref[rows, :] / l_ref[rows, :]).astype(o_ref.dtype)


def dsa_attention(qkv, idx, positions, rel_bias, batch, tq=128, tk=256):
    t = qkv.shape[0]
    s = t // batch
    n_heads = (qkv.shape[1] - 2 * HEAD_DIM) // HEAD_DIM
    tq, tk = min(tq, s), min(tk, s)
    nq, nk = s // tq, s // tk
    topk = min(TOPK_MAX, s // 4)
    qi_w = N_IDX_HEADS * IDX_DIM
    pos_lo = positions.reshape(batch, nq, tq).min(axis=-1)
    pos_hi = positions.reshape(batch, nk, tk).max(axis=-1)
    far = (pos_lo[:, :, None] - pos_hi[:, None, :] >= MAX_DISTANCE).astype(I32).reshape(-1)
    kernel = functools.partial(_dsa_kernel, tq=tq, tk=tk, topk=topk, n_heads=n_heads)
    grid_spec = pltpu.PrefetchScalarGridSpec(
        num_scalar_prefetch=2,
        grid=(batch, nq),
        in_specs=[
            pl.BlockSpec((tq, n_heads * HEAD_DIM), lambda b, i, *_: (b * nq + i, 0)),
            pl.BlockSpec((s, HEAD_DIM), lambda b, i, *_: (b, n_heads)),
            pl.BlockSpec((s, HEAD_DIM), lambda b, i, *_: (b, n_heads + 1)),
            pl.BlockSpec((tq, qi_w), lambda b, i, *_: (b * nq + i, 0)),
            pl.BlockSpec((tq, LANES), lambda b, i, *_: (b * nq + i, qi_w // LANES)),
            pl.BlockSpec((s, LANES), lambda b, i, *_: (b, qi_w // LANES)),
            pl.BlockSpec((tq, 1), lambda b, i, *_: (b * nq + i, 0)),
            pl.BlockSpec((1, 1, s), lambda b, i, *_: (b, 0, 0)),
        ],
        out_specs=pl.BlockSpec((tq, n_heads * HEAD_DIM), lambda b, i, *_: (b * nq + i, 0)),
        scratch_shapes=[
            pltpu.VMEM((tq, s), I32),
            pltpu.VMEM((n_heads * tq, HEAD_DIM), BF16),
            pltpu.VMEM((n_heads * tq, tk), F32),
            pltpu.VMEM((n_heads * tq, tk), BF16),
            pltpu.VMEM((n_heads * tq, 1), F32),
            pltpu.VMEM((n_heads * tq, 1), F32),
            pltpu.VMEM((n_heads * tq, HEAD_DIM), F32),
        ],
    )
    return pl.pallas_call(
        kernel,
        grid_spec=grid_spec,
        out_shape=jax.ShapeDtypeStruct((t, n_heads * HEAD_DIM), BF16),
        compiler_params=_params(("parallel", "arbitrary")),
        name="dsa_attention",
    )(far, rel_bias.reshape(-1), qkv, qkv, qkv, idx, idx, idx,
      positions.reshape(t, 1), positions.reshape(batch, 1, s))


def _mla_kernel(qn_ref, qr_ref, kn_ref, kr_ref, v_ref, o_ref, *, tile):
    i = pl.program_id(2)
    q = jnp.concatenate([qn_ref[...], qr_ref[...]], axis=1)
    row = lax.broadcasted_iota(I32, (tile, tile), 0)
    col = lax.broadcasted_iota(I32, (tile, tile), 1)
    causal = col <= row

    def step(j, m, l, acc, diag):
        sl = pl.ds(pl.multiple_of(j * tile, tile), tile)
        k = jnp.concatenate([kn_ref[sl, :], kr_ref[sl, :]], axis=1)
        s = lax.dot_general(q, k, _NT, preferred_element_type=F32)
        if diag:
            s = jnp.where(causal, s, NEG_INF)
        m_new = jnp.maximum(m, jnp.max(s, axis=1, keepdims=True))
        alpha = jnp.exp(m - m_new)
        p = jnp.exp(s - m_new)
        l = alpha * l + jnp.sum(p, axis=1, keepdims=True)
        acc = alpha * acc + jnp.dot(p.astype(BF16), v_ref[sl, :], preferred_element_type=F32)
        return m_new, l, acc

    init = (jnp.full((tile, 1), NEG_INF, F32), jnp.zeros((tile, 1), F32), jnp.zeros((tile, V_DIM), F32))
    m, l, acc = lax.fori_loop(0, i, lambda j, c: step(j, c[0], c[1], c[2], False), init)
    m, l, acc = step(i, m, l, acc, True)
    o_ref[...] = (acc / l).astype(o_ref.dtype)


def mla_attention(qn, qr, kn, kr, v, batch, tile=256):
    t = qn.shape[0]
    s = t // batch
    h = qn.shape[1] // QK_NOPE
    tile = min(tile, s)
    nq = s // tile
    qspec = pl.BlockSpec((tile, LANES), lambda b, hh, i: (b * nq + i, hh))
    kspec = pl.BlockSpec((s, LANES), lambda b, hh, i: (b, hh))
    return pl.pallas_call(
        functools.partial(_mla_kernel, tile=tile),
        grid=(batch, h, nq),
        in_specs=[qspec, qspec, kspec, pl.BlockSpec((s, LANES), lambda b, hh, i: (b, 0)), kspec],
        out_specs=qspec,
        out_shape=jax.ShapeDtypeStruct((t, h * V_DIM), BF16),
        compiler_params=_params(("parallel", "parallel", "arbitrary")),
        name="mla_attention",
    )(qn, qr, kn, kr, v)


def _first_max(vals):
    best = vals[0]
    for v in vals[1:]:
        best = jnp.maximum(best, v)
    idx = jnp.full(best.shape, len(vals) - 1, I32)
    for k in range(len(vals) - 2, -1, -1):
        idx = jnp.where(vals[k] == best, k, idx)
    return best, idx


def _router_kernel(x_ref, w_ref, b_ref, gates_ref):
    logits = lax.dot_general(w_ref[...], x_ref[...], _NT, precision=lax.Precision.HIGHEST,
                             preferred_element_type=F32)
    aff = jax.nn.sigmoid(logits)
    biased = aff + b_ref[...]
    a_rows = [aff[e:e + 1, :] for e in range(N_EXPERTS)]
    b_rows = [biased[e:e + 1, :] for e in range(N_EXPERTS)]
    g_score, g_e1, g_e2 = [], [], []
    for g in range(N_GROUPS):
        vals = b_rows[g * EXPERTS_PER_GROUP:(g + 1) * EXPERTS_PER_GROUP]
        top1, i1 = _first_max(vals)
        rest = [jnp.where(i1 == k, -jnp.inf, vals[k]) for k in range(EXPERTS_PER_GROUP)]
        top2, i2 = _first_max(rest)
        g_score.append(top1 + top2)
        g_e1.append(i1 + g * EXPERTS_PER_GROUP)
        g_e2.append(i2 + g * EXPERTS_PER_GROUP)
    _, grp = _first_max(g_score)
    e1, e2 = g_e1[N_GROUPS - 1], g_e2[N_GROUPS - 1]
    for g in range(N_GROUPS - 2, -1, -1):
        e1 = jnp.where(grp == g, g_e1[g], e1)
        e2 = jnp.where(grp == g, g_e2[g], e2)
    w1 = jnp.zeros_like(a_rows[0])
    w2 = jnp.zeros_like(a_rows[0])
    for e in range(N_EXPERTS):
        w1 = jnp.where(e1 == e, a_rows[e], w1)
        w2 = jnp.where(e2 == e, a_rows[e], w2)
    tot = w1 + w2
    w1, w2 = w1 / tot, w2 / tot
    for e in range(N_EXPERTS):
        gates_ref[e:e + 1, :] = jnp.where(e1 == e, w1, 0.0) + jnp.where(e2 == e, w2, 0.0)


def router(x, w_router, router_bias, tm=512):
    t, d = x.shape
    tm = min(tm, t)
    return pl.pallas_call(
        _router_kernel,
        grid=(t // tm,),
        in_specs=[pl.BlockSpec((tm, d), lambda i: (i, 0)),
                  pl.BlockSpec((N_EXPERTS, d), lambda i: (0, 0)),
                  pl.BlockSpec((N_EXPERTS, 1), lambda i: (0, 0))],
        out_specs=pl.BlockSpec((N_EXPERTS, tm), lambda i: (0, i)),
        out_shape=jax.ShapeDtypeStruct((N_EXPERTS, t), F32),
        compiler_params=_params(("parallel",)),
        name="router",
    )(x, w_router.T, router_bias.reshape(N_EXPERTS, 1))


def _moe_kernel(x_ref, g_ref, wg_ref, wu_ref, wd_ref, o_ref):
    e = pl.program_id(1)

    @pl.when(e == 0)
    def _():
        o_ref[...] = jnp.zeros_like(o_ref)

    x = x_ref[...]
    h = jax.nn.silu(jnp.dot(x, wg_ref[0], preferred_element_type=F32)) * jnp.dot(
        x, wu_ref[0], preferred_element_type=F32)
    y = jnp.dot(h.astype(BF16), wd_ref[0], preferred_element_type=F32)
    o_ref[...] += g_ref[0] * y


def moe_experts(xb, gates, w_gate, w_up, w_down, tm=256):
    t, d = xb.shape
    e, _, f = w_gate.shape
    tm = min(tm, t)
    return pl.pallas_call(
        _moe_kernel,
        grid=(t // tm, e),
        in_specs=[pl.BlockSpec((tm, d), lambda i, j: (i, 0)),
                  pl.BlockSpec((1, tm, 1), lambda i, j: (j, i, 0)),
                  pl.BlockSpec((1, d, f), lambda i, j: (j, 0, 0)),
                  pl.BlockSpec((1, d, f), lambda i, j: (j, 0, 0)),
                  pl.BlockSpec((1, f, d), lambda i, j: (j, 0, 0))],
        out_specs=pl.BlockSpec((tm, d), lambda i, j: (i, 0)),
        out_shape=jax.ShapeDtypeStruct((t, d), F32),
        compiler_params=_params(("parallel", "arbitrary")),
        name="moe_experts",
    )(xb, gates, w_gate, w_up, w_down)


def even_mixer(xb, positions, w_in, w_out, rel_bias, batch):
    q_scale = HEAD_DIM ** -0.5
    o0, o1, o2, o3, o4, o5, o6, o7 = (SB_W, 2 * SB_W, 3 * SB_W, 3 * SB_W + DSA_W, 3 * SB_W + DSA_W + HEAD_DIM,
                                      3 * SB_W + DSA_W + 2 * HEAD_DIM,
                                      3 * SB_W + DSA_W + 2 * HEAD_DIM + N_IDX_HEADS * IDX_DIM,
                                      3 * SB_W + DSA_W + 2 * HEAD_DIM + N_IDX_HEADS * IDX_DIM + N_IDX_HEADS)
    w_sb = jnp.concatenate([w_in[:, :o0] * q_scale, w_in[:, o0:o2]], axis=1).astype(BF16)
    w_ds = jnp.concatenate([w_in[:, o2:o3] * q_scale, w_in[:, o3:o5]], axis=1).astype(BF16)
    pad = jnp.zeros((w_in.shape[0], LANES - IDX_DIM - N_IDX_HEADS), w_in.dtype)
    w_ix = jnp.concatenate([w_in[:, o5:o6] * IDX_DIM ** -0.5, w_in[:, o7:], w_in[:, o6:o7], pad],
                           axis=1).astype(BF16)
    qkv_sb = matmul(xb, w_sb, BF16, 1024, 512)
    qkv_ds = matmul(xb, w_ds, BF16, 1024, 768)
    idx = matmul(xb, w_ix, F32, 1024, w_ix.shape[1])
    o_sb = sb_attention(qkv_sb, batch)
    o_ds = dsa_attention(qkv_ds, idx, positions, rel_bias, batch)
    w_out = w_out.astype(BF16)
    return matmul2(o_sb, o_ds, w_out[:SB_W], w_out[SB_W:], F32, 1024, 512)


def mla_mixer(xb, positions, w_down, g_q, g_kv, w_uq, w_ukv, w_o, batch):
    t = xb.shape[0]
    scale = (QK_NOPE + QK_ROPE) ** -0.5
    w_down = w_down.astype(BF16)
    kr_pad = jnp.zeros((w_down.shape[0], LANES - QK_ROPE), BF16)
    c_q = matmul(xb, w_down[:, :Q_LORA], F32, 1024, 512)
    c_kv = matmul(xb, w_down[:, Q_LORA:Q_LORA + KV_LORA], F32, 1024, 512)
    k_rope = matmul(xb, jnp.concatenate([w_down[:, Q_LORA + KV_LORA:], kr_pad], axis=1), F32, 1024, LANES)
    c_q = rms_norm(c_q, g_q)
    c_kv = rms_norm(c_kv, g_kv)
    w_uq = (w_uq * scale).reshape(Q_LORA, N_HEADS_MLA, QK_NOPE + QK_ROPE)
    w_qn = w_uq[:, :, :QK_NOPE].reshape(Q_LORA, N_HEADS_MLA * QK_NOPE).astype(BF16)
    w_qr = jnp.pad(w_uq[:, :, QK_NOPE:], ((0, 0), (0, 0), (0, LANES - QK_ROPE)))
    w_qr = w_qr.reshape(Q_LORA, N_HEADS_MLA * LANES).astype(BF16)
    w_ukv = w_ukv.reshape(KV_LORA, N_HEADS_MLA, QK_NOPE + V_DIM)
    w_kn = w_ukv[:, :, :QK_NOPE].reshape(KV_LORA, N_HEADS_MLA * QK_NOPE).astype(BF16)
    w_v = w_ukv[:, :, QK_NOPE:].reshape(KV_LORA, N_HEADS_MLA * V_DIM).astype(BF16)
    qn = matmul(c_q, w_qn, BF16, 1024, 1024)
    qr = matmul(c_q, w_qr, F32, 1024, 1024)
    kn = matmul(c_kv, w_kn, BF16, 1024, 1024)
    v = matmul(c_kv, w_v, BF16, 1024, 1024)
    half = QK_ROPE // 2
    inv_freq = ROPE_THETA ** (-jnp.arange(half, dtype=F32) / half)
    inv_freq = jnp.tile(inv_freq, LANES // half).reshape(1, LANES)
    pos_col = positions.reshape(t, 1)
    qr = rope(qr, pos_col, inv_freq)
    kr = rope(k_rope, pos_col, inv_freq)
    o = mla_attention(qn, qr, kn, kr, v, batch)
    return matmul(o, w_o.astype(BF16), F32, 1024, 512)


def moe(x, xb, w_router, router_bias, w_gate, w_up, w_down):
    t = x.shape[0]
    gates = router(x, w_router, router_bias).reshape(N_EXPERTS, t, 1)
    return moe_experts(xb, gates, w_gate.astype(BF16), w_up.astype(BF16), w_down.astype(BF16))


def kernel(x, positions, rel_bias, even_w_in, even_w_out, mla_w_down, mla_g_q, mla_g_kv, mla_w_uq,
           mla_w_ukv, mla_w_o, w_router, router_bias, exp_w_gate, exp_w_up, exp_w_down, ln_g, ln_b):
    batch, seq, d = x.shape
    x = x.reshape(batch * seq, d)
    xb = x.astype(BF16)
    for layer in range(DEPTH):
        i = layer // 2
        if layer % 2 == 0:
            mix = even_mixer(xb, positions, even_w_in[i], even_w_out[i], rel_bias, batch)
        else:
            mix = mla_mixer(xb, positions, mla_w_down[i], mla_g_q[i], mla_g_kv[i], mla_w_uq[i],
                            mla_w_ukv[i], mla_w_o[i], batch)
        x, xb = deepnorm_ln(x, mix, ln_g[layer, 0], ln_b[layer, 0])
        ffn = moe(x, xb, w_router, router_bias, exp_w_gate[layer], exp_w_up[layer], exp_w_down[layer])
        x, xb = deepnorm_ln(x, ffn, ln_g[layer, 1], ln_b[layer, 1])
    return x.reshape(batch, seq, d)
```

```python
import functools
import math

import jax
import jax.numpy as jnp
from jax import lax
from jax.experimental import pallas as pl
from jax.experimental.pallas import tpu as pltpu

F32 = jnp.float32
BF16 = jnp.bfloat16
I32 = jnp.int32

D_MODEL = 4096
DEPTH = 4
HEAD_DIM = 128
N_HEADS_SB = 16
N_HEADS_DSA = 16
N_IDX_HEADS = 8
IDX_DIM = 64
TOPK_MAX = 256
N_BUCKETS = 32
MAX_DISTANCE = 128
N_HEADS_MLA = 32
Q_LORA = 1024
KV_LORA = 512
QK_NOPE = 128
QK_ROPE = 64
V_DIM = 128
ROPE_THETA = 10000.0
N_EXPERTS = 16
N_GROUPS = 4
EXPERTS_PER_GROUP = N_EXPERTS // N_GROUPS
D_FF_EXPERT = 768
DEEPNORM_ALPHA = (2 * DEPTH) ** 0.25
LN_EPS = 1e-5
RMS_EPS = 1e-6
NEG_INF = -1e30
LOG2E = math.log2(math.e)
SB_W = N_HEADS_SB * HEAD_DIM
DSA_W = N_HEADS_DSA * HEAD_DIM

LANES = 128
INT_MIN = -(2 ** 31)
VMEM_LIMIT = 56 * 1024 * 1024

_NT = (((1,), (1,)), ((), ()))


def _params(sem, vmem=VMEM_LIMIT):
    return pltpu.CompilerParams(dimension_semantics=sem, vmem_limit_bytes=vmem)


def _mm_kernel(a_ref, b_ref, o_ref):
    o_ref[...] = jnp.dot(a_ref[...], b_ref[...], preferred_element_type=F32).astype(o_ref.dtype)


def _mm2_kernel(a1_ref, a2_ref, b1_ref, b2_ref, o_ref):
    acc = jnp.dot(a1_ref[...], b1_ref[...], preferred_element_type=F32)
    acc = acc + jnp.dot(a2_ref[...], b2_ref[...], preferred_element_type=F32)
    o_ref[...] = acc.astype(o_ref.dtype)


def matmul(a, b, out_dtype, tm, tn):
    m, k = a.shape
    n = b.shape[1]
    tm, tn = min(tm, m), min(tn, n)
    assert m % tm == 0 and n % tn == 0
    return pl.pallas_call(
        _mm_kernel,
        grid=(m // tm, n // tn),
        in_specs=[pl.BlockSpec((tm, k), lambda i, j: (i, 0)),
                  pl.BlockSpec((k, tn), lambda i, j: (0, j))],
        out_specs=pl.BlockSpec((tm, tn), lambda i, j: (i, j)),
        out_shape=jax.ShapeDtypeStruct((m, n), out_dtype),
        compiler_params=_params(("parallel", "arbitrary")),
        name="matmul",
    )(a, b)


def matmul2(a1, a2, b1, b2, out_dtype, tm, tn):
    m, k1 = a1.shape
    k2 = a2.shape[1]
    n = b1.shape[1]
    tm, tn = min(tm, m), min(tn, n)
    assert m % tm == 0 and n % tn == 0
    return pl.pallas_call(
        _mm2_kernel,
        grid=(m // tm, n // tn),
        in_specs=[pl.BlockSpec((tm, k1), lambda i, j: (i, 0)),
                  pl.BlockSpec((tm, k2), lambda i, j: (i, 0)),
                  pl.BlockSpec((k1, tn), lambda i, j: (0, j)),
                  pl.BlockSpec((k2, tn), lambda i, j: (0, j))],
        out_specs=pl.BlockSpec((tm, tn), lambda i, j: (i, j)),
        out_shape=jax.ShapeDtypeStruct((m, n), out_dtype),
        compiler_params=_params(("parallel", "arbitrary")),
        name="matmul2",
    )(a1, a2, b1, b2)


def _ln_kernel(x_ref, m_ref, g_ref, b_ref, o_ref, ob_ref):
    h = DEEPNORM_ALPHA * x_ref[...] + m_ref[...]
    mu = jnp.mean(h, axis=-1, keepdims=True)
    d = h - mu
    var = jnp.mean(d * d, axis=-1, keepdims=True)
    y = d * lax.rsqrt(var + LN_EPS) * g_ref[...] + b_ref[...]
    o_ref[...] = y
    ob_ref[...] = y.astype(BF16)


def deepnorm_ln(x, mix, g, b, tm=128):
    t, d = x.shape
    tm = min(tm, t)
    row = pl.BlockSpec((tm, d), lambda i: (i, 0))
    vec = pl.BlockSpec((1, d), lambda i: (0, 0))
    return pl.pallas_call(
        _ln_kernel,
        grid=(t // tm,),
        in_specs=[row, row, vec, vec],
        out_specs=[row, row],
        out_shape=[jax.ShapeDtypeStruct((t, d), F32), jax.ShapeDtypeStruct((t, d), BF16)],
        compiler_params=_params(("parallel",)),
        name="deepnorm_ln",
    )(x, mix, g.reshape(1, d), b.reshape(1, d))


def _rms_kernel(x_ref, g_ref, o_ref):
    x = x_ref[...]
    y = x * lax.rsqrt(jnp.mean(x * x, axis=-1, keepdims=True) + RMS_EPS) * g_ref[...]
    o_ref[...] = y.astype(o_ref.dtype)


def rms_norm(x, g, tm=256):
    t, d = x.shape
    tm = min(tm, t)
    return pl.pallas_call(
        _rms_kernel,
        grid=(t // tm,),
        in_specs=[pl.BlockSpec((tm, d), lambda i: (i, 0)), pl.BlockSpec((1, d), lambda i: (0, 0))],
        out_specs=pl.BlockSpec((tm, d), lambda i: (i, 0)),
        out_shape=jax.ShapeDtypeStruct((t, d), BF16),
        compiler_params=_params(("parallel",)),
        name="rms_norm",
    )(x, g.reshape(1, d))


def _rope_kernel(x_ref, pos_ref, f_ref, o_ref):
    lane = lax.broadcasted_iota(I32, (1, LANES), 1)
    first = (lane % QK_ROPE) < (QK_ROPE // 2)
    ang = pos_ref[...].astype(F32) * f_ref[...]
    cos = jnp.cos(ang)
    sin = jnp.sin(ang)
    sin = jnp.where(first, -sin, sin)
    for c in range(x_ref.shape[1] // LANES):
        x = x_ref[:, c * LANES:(c + 1) * LANES]
        partner = jnp.where(first, pltpu.roll(x, LANES - QK_ROPE // 2, 1), pltpu.roll(x, QK_ROPE // 2, 1))
        o_ref[:, c * LANES:(c + 1) * LANES] = (x * cos + partner * sin).astype(o_ref.dtype)


def rope(x, pos_col, inv_freq_lanes, tm=256):
    t, w = x.shape
    tm = min(tm, t)
    return pl.pallas_call(
        _rope_kernel,
        grid=(t // tm,),
        in_specs=[pl.BlockSpec((tm, w), lambda i: (i, 0)),
                  pl.BlockSpec((tm, 1), lambda i: (i, 0)),
                  pl.BlockSpec((1, LANES), lambda i: (0, 0))],
        out_specs=pl.BlockSpec((tm, w), lambda i: (i, 0)),
        out_shape=jax.ShapeDtypeStruct((t, w), BF16),
        compiler_params=_params(("parallel",)),
        name="rope",
    )(x, pos_col, inv_freq_lanes)


def _sb_kernel(q_ref, k_ref, v_ref, o_ref, *, tile, group):
    i = pl.program_id(2)
    row = lax.broadcasted_iota(I32, (tile, tile), 0)
    col = lax.broadcasted_iota(I32, (tile, tile), 1)
    later = (row > col).astype(BF16)
    strict = col < row
    qs = [q_ref[:, h * HEAD_DIM:(h + 1) * HEAD_DIM] for h in range(group)]

    def step(j, state, diag):
        sl = pl.ds(pl.multiple_of(j * tile, tile), tile)
        heads = range(group)
        hs = [slice(h * HEAD_DIM, (h + 1) * HEAD_DIM) for h in heads]
        z = [lax.dot_general(qs[h], k_ref[sl, hs[h]], _NT, preferred_element_type=F32) for h in heads]
        u = [jnp.log2(1.0 + jnp.exp2(-jnp.abs(z[h]))) for h in heads]
        drop = [jnp.maximum(z[h], 0.0) + u[h] for h in heads]
        if diag:
            drop = [jnp.where(strict, d, 0.0) for d in drop]
        hi = [d.astype(BF16) for d in drop]
        lo = [(drop[h] - hi[h].astype(F32)).astype(BF16) for h in heads]
        after = [jnp.dot(hi[h], later, preferred_element_type=F32)
                 + jnp.dot(lo[h], later, preferred_element_type=F32) for h in heads]
        w = [jnp.exp2(jnp.minimum(z[h], 0.0) - u[h] - after[h] - state[h][0]) for h in heads]
        if diag:
            w = [jnp.where(strict, x, 0.0) for x in w]
        pv = [jnp.dot(w[h].astype(BF16), v_ref[sl, hs[h]], preferred_element_type=F32) for h in heads]
        return tuple((state[h][0] + (after[h][:, 0:1] + drop[h][:, 0:1]), state[h][1] + pv[h]) for h in heads)

    init = tuple((jnp.zeros((tile, 1), F32), jnp.zeros((tile, HEAD_DIM), F32)) for _ in range(group))
    state = step(i, init, True)
    state = lax.fori_loop(0, i, lambda n, st: step(i - 1 - n, st, False), state)
    for h in range(group):
        o_ref[:, h * HEAD_DIM:(h + 1) * HEAD_DIM] = state[h][1].astype(o_ref.dtype)


def sb_attention(qkv, batch, tile=256, group=4):
    t = qkv.shape[0]
    s = t // batch
    h = qkv.shape[1] // (3 * HEAD_DIM)
    tile, group = min(tile, s), min(group, h)
    nq = s // tile
    ng = h // group
    w = group * HEAD_DIM
    return pl.pallas_call(
        functools.partial(_sb_kernel, tile=tile, group=group),
        grid=(batch, ng, nq),
        in_specs=[pl.BlockSpec((tile, w), lambda b, hh, i: (b * nq + i, hh)),
                  pl.BlockSpec((s, w), lambda b, hh, i: (b, ng + hh)),
                  pl.BlockSpec((s, w), lambda b, hh, i: (b, 2 * ng + hh))],
        out_specs=pl.BlockSpec((tile, w), lambda b, hh, i: (b * nq + i, hh)),
        out_shape=jax.ShapeDtypeStruct((t, h * HEAD_DIM), BF16),
        compiler_params=_params(("parallel", "parallel", "arbitrary")),
        name="sb_attention",
    )(qkv, qkv, qkv)


def _t5_bucket(dist):
    n = jnp.maximum(dist, 0)
    max_exact = N_BUCKETS // 2
    scaled = jnp.log(jnp.maximum(n, 1).astype(F32) / max_exact) / math.log(MAX_DISTANCE / max_exact)
    large = jnp.minimum(max_exact + (scaled * (N_BUCKETS - max_exact)).astype(I32), N_BUCKETS - 1)
    return jnp.where(n < max_exact, n, large)


def _dsa_kernel(far_ref,
                q_ref, k_ref, v_ref, qi_ref, wk_q_ref, wk_k_ref, posq_ref, posk_ref, rbt_ref, rbfar_ref,
                o_ref,
                keys_ref, wb_ref, qall_ref, m_ref, acc_ref,
                *, tq, tk, ts, chunk, rb, nk, topk, n_heads, hc, par):
    b = pl.program_id(0)
    i = pl.program_id(1)
    nq = pl.num_programs(1)
    n_causal = (i + 1) * tq
    n_chunks = (n_causal + chunk - 1) // chunk

    qi = qi_ref[...].astype(BF16)
    for hh in range(N_IDX_HEADS):
        wb_ref[hh] = jnp.broadcast_to(wk_q_ref[:, IDX_DIM + hh:IDX_DIM + hh + 1], (tq, ts))

    def to_key(sc):
        sc = jnp.where(sc == 0.0, 0.0, sc)
        bits = pltpu.bitcast(sc, I32)
        return bits ^ ((bits >> 31) & 0x7FFFFFFF)

    neg_key = to_key(jnp.full((tq, ts), NEG_INF, F32))
    t_idx_s = i * tq + lax.broadcasted_iota(I32, (tq, ts), 0)
    col_s = lax.broadcasted_iota(I32, (tq, ts), 1)

    def score_tile(j, _):
        sl = pl.ds(pl.multiple_of(j * ts, ts), ts)

        @pl.when(j * ts < n_causal)
        def _():
            kj = wk_k_ref[sl, 0:IDX_DIM].astype(BF16)
            sc = jnp.zeros((tq, ts), F32)
            for hh in range(N_IDX_HEADS):
                lg = lax.dot_general(qi[:, hh * IDX_DIM:(hh + 1) * IDX_DIM], kj, _NT,
                                     preferred_element_type=F32)
                sc = sc + wb_ref[hh] * jnp.maximum(lg, 0.0)
            sc = sc * (N_IDX_HEADS ** -0.5)
            sc = jnp.where(j * ts + col_s <= t_idx_s, sc, NEG_INF)
            keys_ref[:, sl] = to_key(sc)

        @pl.when(j * ts >= n_causal)
        def _():
            keys_ref[:, sl] = neg_key

        return 0

    lax.fori_loop(0, n_chunks * (chunk // ts), score_tile, 0)

    n_rb = tq // rb

    def count(pred, ref_val):
        ref_b = [jnp.broadcast_to(ref_val[r * rb:(r + 1) * rb, :], (rb, LANES)) for r in range(n_rb)]

        def body(c, accs):
            accs = list(accs)
            for u in range(chunk // LANES):
                sl = pl.ds(pl.multiple_of(c * chunk + u * LANES, LANES), LANES)
                for r in range(n_rb):
                    hit = pred(keys_ref[r * rb:(r + 1) * rb, sl], ref_b[r])
                    accs[r] = accs[r] + jnp.where(hit, 1, 0)
            return tuple(accs)

        accs = lax.fori_loop(0, n_chunks, body, tuple(jnp.zeros((rb, LANES), I32) for _ in range(n_rb)))
        return jnp.concatenate([jnp.sum(a, axis=1, keepdims=True) for a in accs], axis=0)

    def bisect(it, off):
        trial = off | jnp.left_shift(jnp.int32(1), 31 - it)
        cand = trial + jnp.int32(INT_MIN)
        return jnp.where(count(lambda kk, c: kk >= c, cand) >= topk, trial, off)

    thr = lax.fori_loop(0, 32, bisect, jnp.zeros((tq, 1), I32)) + jnp.int32(INT_MIN)
    need = (topk - count(lambda kk, c: kk > c, thr)).astype(F32)

    for hh in range(n_heads):
        qall_ref[hh * tq:(hh + 1) * tq, :] = q_ref[:, hh * HEAD_DIM:(hh + 1) * HEAD_DIM]
    m_ref[...] = jnp.full(m_ref.shape, NEG_INF, F32)
    acc_ref[...] = jnp.zeros(acc_ref.shape, F32)
    r2 = lax.broadcasted_iota(I32, (tk, tk), 0)
    c2 = lax.broadcasted_iota(I32, (tk, tk), 1)
    earlier = (r2 < c2).astype(BF16)
    ones_col = (lax.broadcasted_iota(I32, (tk, LANES), 1) == 0).astype(BF16)
    pos_q = posq_ref[...]
    n_hc = n_heads // hc
    rows_c = hc * tq
    n_tiles = (n_causal + tk - 1) // tk
    t_idx = i * tq + lax.broadcasted_iota(I32, (tq, tk), 0)
    col = lax.broadcasted_iota(I32, (tq, tk), 1)

    def attend_tile(j, eq_before):
        sl = pl.ds(pl.multiple_of(j * tk, tk), tk)
        kt = keys_ref[:, sl]
        eq = jnp.where(kt == thr, 1.0, 0.0)
        rank = eq_before + jnp.dot(eq.astype(BF16), earlier, preferred_element_type=F32)
        take = jnp.where(kt > thr, 1.0, jnp.where(rank < need, eq, 0.0))
        mask = jnp.where(j * tk + col <= t_idx, take, 0.0) > 0.5
        is_far = far_ref[(b * nq + i) * nk + j] == 1
        k_t = k_ref[sl, :]
        v_ext = jnp.concatenate([v_ref[sl, :], ones_col], axis=1)

        def near_bias(c):
            bucket = _t5_bucket(pos_q - posk_ref[0, :, sl])
            per_head = []
            for hh in range(c * hc, (c + 1) * hc):
                tbl = jnp.broadcast_to(rbt_ref[hh:hh + 1, :], (tq, LANES))
                halves = [jnp.take_along_axis(tbl, bucket[:, x * LANES:(x + 1) * LANES], axis=1,
                                              mode="promise_in_bounds") for x in range(tk // LANES)]
                per_head.append(jnp.concatenate(halves, axis=1))
            return jnp.stack(per_head, axis=0)

        for g in range(n_hc // par):
            chunks = range(par)
            rows = [slice((g * par + c) * rows_c, (g * par + c + 1) * rows_c) for c in chunks]
            s = [lax.dot_general(qall_ref[rows[c], :], k_t, _NT, preferred_element_type=F32).reshape(hc, tq, tk)
                 for c in chunks]
            s = [lax.cond(is_far,
                          lambda x=s[c]: jnp.where(mask[None], x, NEG_INF),
                          lambda x=s[c], c=c: jnp.where(mask[None], x + near_bias(g * par + c), NEG_INF)
                          ).reshape(rows_c, tk)
                 for c in chunks]
            shift = [jnp.where(is_far, rbfar_ref[rows[c], :], 0.0) for c in chunks]
            m_old = [m_ref[rows[c], :] for c in chunks]
            m_new = [jnp.maximum(m_old[c], jnp.max(s[c], axis=1, keepdims=True) + shift[c]) for c in chunks]
            p = [jnp.exp2(s[c] - (m_new[c] - shift[c])).astype(BF16) for c in chunks]
            pv = [jnp.dot(p[c], v_ext, preferred_element_type=F32) for c in chunks]
            for c in chunks:
                acc_ref[rows[c], :] = jnp.exp2(m_old[c] - m_new[c]) * acc_ref[rows[c], :] + pv[c]
                m_ref[rows[c], :] = m_new[c]
        return eq_before + jnp.sum(eq, axis=1, keepdims=True)

    lax.fori_loop(0, n_tiles, attend_tile, jnp.zeros((tq, 1), F32))
    for hh in range(n_heads):
        rows = slice(hh * tq, (hh + 1) * tq)
        o_ref[:, hh * HEAD_DIM:(hh + 1) * HEAD_DIM] = (
            acc_ref[rows, 0:HEAD_DIM] / acc_ref[rows, HEAD_DIM:HEAD_DIM + 1]).astype(o_ref.dtype)


def dsa_attention(qkv, idx, positions, rel_bias, batch, tq=256, tk=512, ts=256, chunk=1024, rb=128, hc=4,
                  par=2):
    t = qkv.shape[0]
    s = t // batch
    n_heads = (qkv.shape[1] - 2 * HEAD_DIM) // HEAD_DIM
    tq, tk, ts, chunk, rb = min(tq, s), min(tk, s), min(ts, s), min(chunk, s), min(rb, tq)
    assert chunk % tk == 0 and chunk % ts == 0 and s % chunk == 0 and tq % rb == 0
    hc = min(hc, n_heads)
    par = min(par, n_heads // hc)
    nq, nk = s // tq, s // tk
    topk = min(TOPK_MAX, s // 4)
    qi_w = N_IDX_HEADS * IDX_DIM
    pos_lo = positions.reshape(batch, nq, tq).min(axis=-1)
    pos_hi = positions.reshape(batch, nk, tk).max(axis=-1)
    far = (pos_lo[:, :, None] - pos_hi[:, None, :] >= MAX_DISTANCE).astype(I32).reshape(-1)
    rb2 = rel_bias.astype(F32) * LOG2E
    rbt = jnp.pad(rb2.T, ((0, 0), (0, LANES - N_BUCKETS)))
    rbfar = jnp.repeat(rb2[N_BUCKETS - 1], tq).reshape(n_heads * tq, 1)
    kernel = functools.partial(_dsa_kernel, tq=tq, tk=tk, ts=ts, chunk=chunk, rb=rb, nk=nk, topk=topk,
                               n_heads=n_heads, hc=hc, par=par)
    grid_spec = pltpu.PrefetchScalarGridSpec(
        num_scalar_prefetch=1,
        grid=(batch, nq),
        in_specs=[
            pl.BlockSpec((tq, n_heads * HEAD_DIM), lambda b, i, *_: (b * nq + i, 0)),
            pl.BlockSpec((s, HEAD_DIM), lambda b, i, *_: (b, n_heads)),
            pl.BlockSpec((s, HEAD_DIM), lambda b, i, *_: (b, n_heads + 1)),
            pl.BlockSpec((tq, qi_w), lambda b, i, *_: (b * nq + i, 0)),
            pl.BlockSpec((tq, LANES), lambda b, i, *_: (b * nq + i, qi_w // LANES)),
            pl.BlockSpec((s, LANES), lambda b, i, *_: (b, qi_w // LANES)),
            pl.BlockSpec((tq, 1), lambda b, i, *_: (b * nq + i, 0)),
            pl.BlockSpec((1, 1, s), lambda b, i, *_: (b, 0, 0)),
            pl.BlockSpec((n_heads, LANES), lambda b, i, *_: (0, 0)),
            pl.BlockSpec((n_heads * tq, 1), lambda b, i, *_: (0, 0)),
        ],
        out_specs=pl.BlockSpec((tq, n_heads * HEAD_DIM), lambda b, i, *_: (b * nq + i, 0)),
        scratch_shapes=[
            pltpu.VMEM((tq, s), I32),
            pltpu.VMEM((N_IDX_HEADS, tq, ts), F32),
            pltpu.VMEM((n_heads * tq, HEAD_DIM), BF16),
            pltpu.VMEM((n_heads * tq, 1), F32),
            pltpu.VMEM((n_heads * tq, 2 * HEAD_DIM), F32),
        ],
    )
    return pl.pallas_call(
        kernel,
        grid_spec=grid_spec,
        out_shape=jax.ShapeDtypeStruct((t, n_heads * HEAD_DIM), BF16),
        compiler_params=_params(("parallel", "arbitrary")),
        name="dsa_attention",
    )(far, qkv, qkv, qkv, idx, idx, idx, positions.reshape(t, 1), positions.reshape(batch, 1, s),
      rbt, rbfar)


def _mla_kernel(qn_ref, qr_ref, kn_ref, kr_ref, v_ref, o_ref, *, tq, tkb, group):
    i = pl.program_id(2)
    row = lax.broadcasted_iota(I32, (tq, tkb), 0)
    col = lax.broadcasted_iota(I32, (tq, tkb), 1)
    ones_col = (lax.broadcasted_iota(I32, (tkb, LANES), 1) == 0).astype(BF16)
    qs = [jnp.concatenate([qn_ref[:, h * LANES:(h + 1) * LANES], qr_ref[:, h * LANES:(h + 1) * LANES]], axis=1)
          for h in range(group)]
    n_full = (i * tq) // tkb

    def chunk(c, state, masked):
        sl = pl.ds(pl.multiple_of(c * tkb, tkb), tkb)
        kr = kr_ref[sl, :]
        heads = range(group)
        hs = [slice(h * LANES, (h + 1) * LANES) for h in heads]
        s = [lax.dot_general(qs[h], jnp.concatenate([kn_ref[sl, hs[h]], kr], axis=1), _NT,
                             preferred_element_type=F32) for h in heads]
        if masked:
            s = [jnp.where(c * tkb + col <= i * tq + row, x, NEG_INF) for x in s]
        m_new = [jnp.maximum(state[h][0], jnp.max(s[h], axis=1, keepdims=True)) for h in heads]
        p = [jnp.exp2(s[h] - m_new[h]).astype(BF16) for h in heads]
        pv = [jnp.dot(p[h], jnp.concatenate([v_ref[sl, hs[h]], ones_col], axis=1),
                      preferred_element_type=F32) for h in heads]
        return tuple((m_new[h], jnp.exp2(state[h][0] - m_new[h]) * state[h][1] + pv[h]) for h in heads)

    init = tuple((jnp.full((tq, 1), NEG_INF, F32), jnp.zeros((tq, 2 * LANES), F32)) for _ in range(group))
    state = lax.fori_loop(0, n_full, lambda c, st: chunk(c, st, False), init)
    state = chunk(n_full, state, True)
    for h in range(group):
        acc = state[h][1]
        o_ref[:, h * LANES:(h + 1) * LANES] = (acc[:, 0:V_DIM] / acc[:, V_DIM:V_DIM + 1]).astype(o_ref.dtype)


def mla_attention(qn, qr, kn, kr, v, batch, tq=256, tkb=512, group=4):
    t = qn.shape[0]
    s = t // batch
    h = qn.shape[1] // QK_NOPE
    tq, tkb, group = min(tq, s), min(tkb, s), min(group, h)
    nq = s // tq
    w = group * LANES
    qspec = pl.BlockSpec((tq, w), lambda b, hh, i: (b * nq + i, hh))
    kspec = pl.BlockSpec((s, w), lambda b, hh, i: (b, hh))
    return pl.pallas_call(
        functools.partial(_mla_kernel, tq=tq, tkb=tkb, group=group),
        grid=(batch, h // group, nq),
        in_specs=[qspec, qspec, kspec, pl.BlockSpec((s, LANES), lambda b, hh, i: (b, 0)), kspec],
        out_specs=qspec,
        out_shape=jax.ShapeDtypeStruct((t, h * V_DIM), BF16),
        compiler_params=_params(("parallel", "parallel", "arbitrary")),
        name="mla_attention",
    )(qn, qr, kn, kr, v)


def _first_max(vals):
    best = vals[0]
    for v in vals[1:]:
        best = jnp.maximum(best, v)
    idx = jnp.full(best.shape, len(vals) - 1, I32)
    for k in range(len(vals) - 2, -1, -1):
        idx = jnp.where(vals[k] == best, k, idx)
    return best, idx


def _router_kernel(x_ref, w_ref, b_ref, gates_ref):
    logits = lax.dot_general(w_ref[...], x_ref[...], _NT, precision=lax.Precision.HIGHEST,
                             preferred_element_type=F32)
    aff = jax.nn.sigmoid(logits)
    biased = aff + b_ref[...]
    a_rows = [aff[e:e + 1, :] for e in range(N_EXPERTS)]
    b_rows = [biased[e:e + 1, :] for e in range(N_EXPERTS)]
    g_score, g_e1, g_e2 = [], [], []
    for g in range(N_GROUPS):
        vals = b_rows[g * EXPERTS_PER_GROUP:(g + 1) * EXPERTS_PER_GROUP]
        top1, i1 = _first_max(vals)
        rest = [jnp.where(i1 == k, -jnp.inf, vals[k]) for k in range(EXPERTS_PER_GROUP)]
        top2, i2 = _first_max(rest)
        g_score.append(top1 + top2)
        g_e1.append(i1 + g * EXPERTS_PER_GROUP)
        g_e2.append(i2 + g * EXPERTS_PER_GROUP)
    _, grp = _first_max(g_score)
    e1, e2 = g_e1[N_GROUPS - 1], g_e2[N_GROUPS - 1]
    for g in range(N_GROUPS - 2, -1, -1):
        e1 = jnp.where(grp == g, g_e1[g], e1)
        e2 = jnp.where(grp == g, g_e2[g], e2)
    w1 = jnp.zeros_like(a_rows[0])
    w2 = jnp.zeros_like(a_rows[0])
    for e in range(N_EXPERTS):
        w1 = jnp.where(e1 == e, a_rows[e], w1)
        w2 = jnp.where(e2 == e, a_rows[e], w2)
    tot = w1 + w2
    w1, w2 = w1 / tot, w2 / tot
    for e in range(N_EXPERTS):
        gates_ref[e:e + 1, :] = jnp.where(e1 == e, w1, 0.0) + jnp.where(e2 == e, w2, 0.0)


def router(x, w_router, router_bias, tm=512):
    t, d = x.shape
    tm = min(tm, t)
    return pl.pallas_call(
        _router_kernel,
        grid=(t // tm,),
        in_specs=[pl.BlockSpec((tm, d), lambda i: (i, 0)),
                  pl.BlockSpec((N_EXPERTS, d), lambda i: (0, 0)),
                  pl.BlockSpec((N_EXPERTS, 1), lambda i: (0, 0))],
        out_specs=pl.BlockSpec((N_EXPERTS, tm), lambda i: (0, i)),
        out_shape=jax.ShapeDtypeStruct((N_EXPERTS, t), F32),
        compiler_params=_params(("parallel",)),
        name="router",
    )(x, w_router.T, router_bias.reshape(N_EXPERTS, 1))


def _moe_kernel(x_ref, g_ref, wg_ref, wu_ref, wd_ref, o_ref):
    e = pl.program_id(1)

    @pl.when(e == 0)
    def _():
        o_ref[...] = jnp.zeros_like(o_ref)

    x = x_ref[...]
    h = jax.nn.silu(jnp.dot(x, wg_ref[0], preferred_element_type=F32)) * jnp.dot(
        x, wu_ref[0], preferred_element_type=F32)
    y = jnp.dot(h.astype(BF16), wd_ref[0], preferred_element_type=F32)
    o_ref[...] += g_ref[0] * y


def moe_experts(xb, gates, w_gate, w_up, w_down, tm=256):
    t, d = xb.shape
    e, _, f = w_gate.shape
    tm = min(tm, t)
    return pl.pallas_call(
        _moe_kernel,
        grid=(t // tm, e),
        in_specs=[pl.BlockSpec((tm, d), lambda i, j: (i, 0)),
                  pl.BlockSpec((1, tm, 1), lambda i, j: (j, i, 0)),
                  pl.BlockSpec((1, d, f), lambda i, j: (j, 0, 0)),
                  pl.BlockSpec((1, d, f), lambda i, j: (j, 0, 0)),
                  pl.BlockSpec((1, f, d), lambda i, j: (j, 0, 0))],
        out_specs=pl.BlockSpec((tm, d), lambda i, j: (i, 0)),
        out_shape=jax.ShapeDtypeStruct((t, d), F32),
        compiler_params=_params(("parallel", "arbitrary")),
        name="moe_experts",
    )(xb, gates, w_gate, w_up, w_down)


def even_mixer(xb, positions, w_in, w_out, rel_bias, batch):
    q_scale = HEAD_DIM ** -0.5 * LOG2E
    o0, o1, o2, o3, o4, o5, o6, o7 = (SB_W, 2 * SB_W, 3 * SB_W, 3 * SB_W + DSA_W, 3 * SB_W + DSA_W + HEAD_DIM,
                                      3 * SB_W + DSA_W + 2 * HEAD_DIM,
                                      3 * SB_W + DSA_W + 2 * HEAD_DIM + N_IDX_HEADS * IDX_DIM,
                                      3 * SB_W + DSA_W + 2 * HEAD_DIM + N_IDX_HEADS * IDX_DIM + N_IDX_HEADS)
    w_sb = jnp.concatenate([w_in[:, :o0] * q_scale, w_in[:, o0:o2]], axis=1).astype(BF16)
    w_ds = jnp.concatenate([w_in[:, o2:o3] * q_scale, w_in[:, o3:o5]], axis=1).astype(BF16)
    pad = jnp.zeros((w_in.shape[0], LANES - IDX_DIM - N_IDX_HEADS), w_in.dtype)
    w_ix = jnp.concatenate([w_in[:, o5:o6] * IDX_DIM ** -0.5, w_in[:, o7:], w_in[:, o6:o7], pad],
                           axis=1).astype(BF16)
    qkv_sb = matmul(xb, w_sb, BF16, 1024, 512)
    qkv_ds = matmul(xb, w_ds, BF16, 1024, 768)
    idx = matmul(xb, w_ix, F32, 1024, w_ix.shape[1])
    o_sb = sb_attention(qkv_sb, batch)
    o_ds = dsa_attention(qkv_ds, idx, positions, rel_bias, batch)
    w_out = w_out.astype(BF16)
    return matmul2(o_sb, o_ds, w_out[:SB_W], w_out[SB_W:], F32, 1024, 512)


def mla_mixer(xb, positions, w_down, g_q, g_kv, w_uq, w_ukv, w_o, batch):
    t = xb.shape[0]
    scale = (QK_NOPE + QK_ROPE) ** -0.5 * LOG2E
    w_down = w_down.astype(BF16)
    kr_pad = jnp.zeros((w_down.shape[0], LANES - QK_ROPE), BF16)
    c_q = matmul(xb, w_down[:, :Q_LORA], F32, 1024, 512)
    c_kv = matmul(xb, w_down[:, Q_LORA:Q_LORA + KV_LORA], F32, 1024, 512)
    k_rope = matmul(xb, jnp.concatenate([w_down[:, Q_LORA + KV_LORA:], kr_pad], axis=1), F32, 1024, LANES)
    c_q = rms_norm(c_q, g_q)
    c_kv = rms_norm(c_kv, g_kv)
    w_uq = (w_uq * scale).reshape(Q_LORA, N_HEADS_MLA, QK_NOPE + QK_ROPE)
    w_qn = w_uq[:, :, :QK_NOPE].reshape(Q_LORA, N_HEADS_MLA * QK_NOPE).astype(BF16)
    w_qr = jnp.pad(w_uq[:, :, QK_NOPE:], ((0, 0), (0, 0), (0, LANES - QK_ROPE)))
    w_qr = w_qr.reshape(Q_LORA, N_HEADS_MLA * LANES).astype(BF16)
    w_ukv = w_ukv.reshape(KV_LORA, N_HEADS_MLA, QK_NOPE + V_DIM)
    w_kn = w_ukv[:, :, :QK_NOPE].reshape(KV_LORA, N_HEADS_MLA * QK_NOPE).astype(BF16)
    w_v = w_ukv[:, :, QK_NOPE:].reshape(KV_LORA, N_HEADS_MLA * V_DIM).astype(BF16)
    qn = matmul(c_q, w_qn, BF16, 1024, 1024)
    qr = matmul(c_q, w_qr, F32, 1024, 1024)
    kn = matmul(c_kv, w_kn, BF16, 1024, 1024)
    v = matmul(c_kv, w_v, BF16, 1024, 1024)
    half = QK_ROPE // 2
    inv_freq = ROPE_THETA ** (-jnp.arange(half, dtype=F32) / half)
    inv_freq = jnp.tile(inv_freq, LANES // half).reshape(1, LANES)
    pos_col = positions.reshape(t, 1)
    qr = rope(qr, pos_col, inv_freq)
    kr = rope(k_rope, pos_col, inv_freq)
    o = mla_attention(qn, qr, kn, kr, v, batch)
    return matmul(o, w_o.astype(BF16), F32, 1024, 512)


def moe(x, xb, w_router, router_bias, w_gate, w_up, w_down):
    t = x.shape[0]
    gates = router(x, w_router, router_bias).reshape(N_EXPERTS, t, 1)
    return moe_experts(xb, gates, w_gate.astype(BF16), w_up.astype(BF16), w_down.astype(BF16))


def kernel(x, positions, rel_bias, even_w_in, even_w_out, mla_w_down, mla_g_q, mla_g_kv, mla_w_uq,
           mla_w_ukv, mla_w_o, w_router, router_bias, exp_w_gate, exp_w_up, exp_w_down, ln_g, ln_b):
    batch, seq, d = x.shape
    x = x.reshape(batch * seq, d)
    xb = x.astype(BF16)
    for layer in range(DEPTH):
        i = layer // 2
        if layer % 2 == 0:
            mix = even_mixer(xb, positions, even_w_in[i], even_w_out[i], rel_bias, batch)
        else:
            mix = mla_mixer(xb, positions, mla_w_down[i], mla_g_q[i], mla_g_kv[i], mla_w_uq[i],
                            mla_w_ukv[i], mla_w_o[i], batch)
        x, xb = deepnorm_ln(x, mix, ln_g[layer, 0], ln_b[layer, 0])
        ffn = moe(x, xb, w_router, router_bias, exp_w_gate[layer], exp_w_up[layer], exp_w_down[layer])
        x, xb = deepnorm_ln(x, ffn, ln_g[layer, 1], ln_b[layer, 1])
    return x.reshape(batch, seq, d)
```

```python
import functools
import math

import jax
import jax.numpy as jnp
from jax import lax
from jax.experimental import pallas as pl
from jax.experimental.pallas import tpu as pltpu

F32 = jnp.float32
BF16 = jnp.bfloat16
I32 = jnp.int32

D_MODEL = 4096
DEPTH = 4
HEAD_DIM = 128
N_HEADS_SB = 16
N_HEADS_DSA = 16
N_IDX_HEADS = 8
IDX_DIM = 64
TOPK_MAX = 256
N_BUCKETS = 32
MAX_DISTANCE = 128
N_HEADS_MLA = 32
Q_LORA = 1024
KV_LORA = 512
QK_NOPE = 128
QK_ROPE = 64
V_DIM = 128
ROPE_THETA = 10000.0
N_EXPERTS = 16
N_GROUPS = 4
EXPERTS_PER_GROUP = N_EXPERTS // N_GROUPS
TOP_K_EXPERTS = 2
D_FF_EXPERT = 768
DEEPNORM_ALPHA = (2 * DEPTH) ** 0.25
LN_EPS = 1e-5
RMS_EPS = 1e-6
NEG_INF = -1e30
LOG2E = math.log2(math.e)
SB_W = N_HEADS_SB * HEAD_DIM
DSA_W = N_HEADS_DSA * HEAD_DIM

LANES = 128
INT_MIN = -(2 ** 31)
VMEM_LIMIT = 56 * 1024 * 1024

_NT = (((1,), (1,)), ((), ()))


def _params(sem, vmem=VMEM_LIMIT):
    return pltpu.CompilerParams(dimension_semantics=sem, vmem_limit_bytes=vmem)


def _mm_kernel(a_ref, b_ref, o_ref):
    o_ref[...] = jnp.dot(a_ref[...], b_ref[...], preferred_element_type=F32).astype(o_ref.dtype)


def _mm2_kernel(a1_ref, a2_ref, b1_ref, b2_ref, o_ref):
    acc = jnp.dot(a1_ref[...], b1_ref[...], preferred_element_type=F32)
    acc = acc + jnp.dot(a2_ref[...], b2_ref[...], preferred_element_type=F32)
    o_ref[...] = acc.astype(o_ref.dtype)


def matmul(a, b, out_dtype, tm, tn):
    m, k = a.shape
    n = b.shape[1]
    tm, tn = min(tm, m), min(tn, n)
    assert m % tm == 0 and n % tn == 0
    return pl.pallas_call(
        _mm_kernel,
        grid=(m // tm, n // tn),
        in_specs=[pl.BlockSpec((tm, k), lambda i, j: (i, 0)),
                  pl.BlockSpec((k, tn), lambda i, j: (0, j))],
        out_specs=pl.BlockSpec((tm, tn), lambda i, j: (i, j)),
        out_shape=jax.ShapeDtypeStruct((m, n), out_dtype),
        compiler_params=_params(("parallel", "arbitrary")),
        name="matmul",
    )(a, b)


def matmul2(a1, a2, b1, b2, out_dtype, tm, tn):
    m, k1 = a1.shape
    k2 = a2.shape[1]
    n = b1.shape[1]
    tm, tn = min(tm, m), min(tn, n)
    assert m % tm == 0 and n % tn == 0
    return pl.pallas_call(
        _mm2_kernel,
        grid=(m // tm, n // tn),
        in_specs=[pl.BlockSpec((tm, k1), lambda i, j: (i, 0)),
                  pl.BlockSpec((tm, k2), lambda i, j: (i, 0)),
                  pl.BlockSpec((k1, tn), lambda i, j: (0, j)),
                  pl.BlockSpec((k2, tn), lambda i, j: (0, j))],
        out_specs=pl.BlockSpec((tm, tn), lambda i, j: (i, j)),
        out_shape=jax.ShapeDtypeStruct((m, n), out_dtype),
        compiler_params=_params(("parallel", "arbitrary")),
        name="matmul2",
    )(a1, a2, b1, b2)


def _ln_kernel(x_ref, m_ref, g_ref, b_ref, o_ref, ob_ref):
    h = DEEPNORM_ALPHA * x_ref[...] + m_ref[...]
    mu = jnp.mean(h, axis=-1, keepdims=True)
    d = h - mu
    var = jnp.mean(d * d, axis=-1, keepdims=True)
    y = d * lax.rsqrt(var + LN_EPS) * g_ref[...] + b_ref[...]
    o_ref[...] = y
    ob_ref[...] = y.astype(BF16)


def deepnorm_ln(x, mix, g, b, tm=128):
    t, d = x.shape
    tm = min(tm, t)
    row = pl.BlockSpec((tm, d), lambda i: (i, 0))
    vec = pl.BlockSpec((1, d), lambda i: (0, 0))
    return pl.pallas_call(
        _ln_kernel,
        grid=(t // tm,),
        in_specs=[row, row, vec, vec],
        out_specs=[row, row],
        out_shape=[jax.ShapeDtypeStruct((t, d), F32), jax.ShapeDtypeStruct((t, d), BF16)],
        compiler_params=_params(("parallel",)),
        name="deepnorm_ln",
    )(x, mix, g.reshape(1, d), b.reshape(1, d))


def _rms_kernel(x_ref, g_ref, o_ref):
    x = x_ref[...]
    y = x * lax.rsqrt(jnp.mean(x * x, axis=-1, keepdims=True) + RMS_EPS) * g_ref[...]
    o_ref[...] = y.astype(o_ref.dtype)


def rms_norm(x, g, tm=256):
    t, d = x.shape
    tm = min(tm, t)
    return pl.pallas_call(
        _rms_kernel,
        grid=(t // tm,),
        in_specs=[pl.BlockSpec((tm, d), lambda i: (i, 0)), pl.BlockSpec((1, d), lambda i: (0, 0))],
        out_specs=pl.BlockSpec((tm, d), lambda i: (i, 0)),
        out_shape=jax.ShapeDtypeStruct((t, d), BF16),
        compiler_params=_params(("parallel",)),
        name="rms_norm",
    )(x, g.reshape(1, d))


def _rope_kernel(x_ref, pos_ref, f_ref, o_ref):
    lane = lax.broadcasted_iota(I32, (1, LANES), 1)
    first = (lane % QK_ROPE) < (QK_ROPE // 2)
    ang = pos_ref[...].astype(F32) * f_ref[...]
    cos = jnp.cos(ang)
    sin = jnp.sin(ang)
    sin = jnp.where(first, -sin, sin)
    for c in range(x_ref.shape[1] // LANES):
        x = x_ref[:, c * LANES:(c + 1) * LANES]
        partner = jnp.where(first, pltpu.roll(x, LANES - QK_ROPE // 2, 1), pltpu.roll(x, QK_ROPE // 2, 1))
        o_ref[:, c * LANES:(c + 1) * LANES] = (x * cos + partner * sin).astype(o_ref.dtype)


def rope(x, pos_col, inv_freq_lanes, tm=256):
    t, w = x.shape
    tm = min(tm, t)
    return pl.pallas_call(
        _rope_kernel,
        grid=(t // tm,),
        in_specs=[pl.BlockSpec((tm, w), lambda i: (i, 0)),
                  pl.BlockSpec((tm, 1), lambda i: (i, 0)),
                  pl.BlockSpec((1, LANES), lambda i: (0, 0))],
        out_specs=pl.BlockSpec((tm, w), lambda i: (i, 0)),
        out_shape=jax.ShapeDtypeStruct((t, w), BF16),
        compiler_params=_params(("parallel",)),
        name="rope",
    )(x, pos_col, inv_freq_lanes)


def _sb_kernel(q_ref, k_ref, v_ref, o_ref, *, tile, group):
    i = pl.program_id(2)
    row = lax.broadcasted_iota(I32, (tile, tile), 0)
    col = lax.broadcasted_iota(I32, (tile, tile), 1)
    later = (row > col).astype(BF16)
    strict = col < row
    qs = [q_ref[:, h * HEAD_DIM:(h + 1) * HEAD_DIM] for h in range(group)]

    def step(j, state, diag):
        sl = pl.ds(pl.multiple_of(j * tile, tile), tile)
        heads = range(group)
        hs = [slice(h * HEAD_DIM, (h + 1) * HEAD_DIM) for h in heads]
        z = [lax.dot_general(qs[h], k_ref[sl, hs[h]], _NT, preferred_element_type=F32) for h in heads]
        u = [jnp.log2(1.0 + jnp.exp2(-jnp.abs(z[h]))) for h in heads]
        drop = [jnp.maximum(z[h], 0.0) + u[h] for h in heads]
        if diag:
            drop = [jnp.where(strict, d, 0.0) for d in drop]
        hi = [d.astype(BF16) for d in drop]
        lo = [(drop[h] - hi[h].astype(F32)).astype(BF16) for h in heads]
        after = [jnp.dot(hi[h], later, preferred_element_type=F32)
                 + jnp.dot(lo[h], later, preferred_element_type=F32) for h in heads]
        w = [jnp.exp2(jnp.minimum(z[h], 0.0) - u[h] - after[h] - state[h][0]) for h in heads]
        if diag:
            w = [jnp.where(strict, x, 0.0) for x in w]
        pv = [jnp.dot(w[h].astype(BF16), v_ref[sl, hs[h]], preferred_element_type=F32) for h in heads]
        return tuple((state[h][0] + (after[h][:, 0:1] + drop[h][:, 0:1]), state[h][1] + pv[h]) for h in heads)

    init = tuple((jnp.zeros((tile, 1), F32), jnp.zeros((tile, HEAD_DIM), F32)) for _ in range(group))
    state = step(i, init, True)
    state = lax.fori_loop(0, i, lambda n, st: step(i - 1 - n, st, False), state)
    for h in range(group):
        o_ref[:, h * HEAD_DIM:(h + 1) * HEAD_DIM] = state[h][1].astype(o_ref.dtype)


def sb_attention(qkv, batch, tile=256, group=4):
    t = qkv.shape[0]
    s = t // batch
    h = qkv.shape[1] // (3 * HEAD_DIM)
    tile, group = min(tile, s), min(group, h)
    nq = s // tile
    ng = h // group
    w = group * HEAD_DIM
    return pl.pallas_call(
        functools.partial(_sb_kernel, tile=tile, group=group),
        grid=(batch, ng, nq),
        in_specs=[pl.BlockSpec((tile, w), lambda b, hh, i: (b * nq + i, hh)),
                  pl.BlockSpec((s, w), lambda b, hh, i: (b, ng + hh)),
                  pl.BlockSpec((s, w), lambda b, hh, i: (b, 2 * ng + hh))],
        out_specs=pl.BlockSpec((tile, w), lambda b, hh, i: (b * nq + i, hh)),
        out_shape=jax.ShapeDtypeStruct((t, h * HEAD_DIM), BF16),
        compiler_params=_params(("parallel", "parallel", "arbitrary")),
        name="sb_attention",
    )(qkv, qkv, qkv)


def _t5_bucket(dist):
    n = jnp.maximum(dist, 0)
    max_exact = N_BUCKETS // 2
    scaled = jnp.log(jnp.maximum(n, 1).astype(F32) / max_exact) / math.log(MAX_DISTANCE / max_exact)
    large = jnp.minimum(max_exact + (scaled * (N_BUCKETS - max_exact)).astype(I32), N_BUCKETS - 1)
    return jnp.where(n < max_exact, n, large)


def _dsa_kernel(far_ref,
                q_ref, k_ref, v_ref, qi_ref, wk_q_ref, wk_k_ref, posq_ref, posk_ref, rbt_ref, rbfar_ref,
                o_ref,
                keys_ref, wb_ref, qall_ref, m_ref, acc_ref,
                *, tq, tk, ts, chunk, rb, nk, topk, n_heads, hc, par):
    b = pl.program_id(0)
    i = pl.program_id(1)
    nq = pl.num_programs(1)
    n_causal = (i + 1) * tq
    n_chunks = (n_causal + chunk - 1) // chunk

    qi = qi_ref[...].astype(BF16)
    for hh in range(N_IDX_HEADS):
        wb_ref[hh] = jnp.broadcast_to(wk_q_ref[:, IDX_DIM + hh:IDX_DIM + hh + 1], (tq, ts))

    def to_key(sc):
        sc = jnp.where(sc == 0.0, 0.0, sc)
        bits = pltpu.bitcast(sc, I32)
        return bits ^ ((bits >> 31) & 0x7FFFFFFF)

    neg_key = to_key(jnp.full((tq, ts), NEG_INF, F32))
    t_idx_s = i * tq + lax.broadcasted_iota(I32, (tq, ts), 0)
    col_s = lax.broadcasted_iota(I32, (tq, ts), 1)

    def score_tile(j, _):
        sl = pl.ds(pl.multiple_of(j * ts, ts), ts)

        @pl.when(j * ts < n_causal)
        def _():
            kj = wk_k_ref[sl, 0:IDX_DIM].astype(BF16)
            sc = jnp.zeros((tq, ts), F32)
            for hh in range(N_IDX_HEADS):
                lg = lax.dot_general(qi[:, hh * IDX_DIM:(hh + 1) * IDX_DIM], kj, _NT,
                                     preferred_element_type=F32)
                sc = sc + wb_ref[hh] * jnp.maximum(lg, 0.0)
            sc = sc * (N_IDX_HEADS ** -0.5)
            sc = jnp.where(j * ts + col_s <= t_idx_s, sc, NEG_INF)
            keys_ref[:, sl] = to_key(sc)

        @pl.when(j * ts >= n_causal)
        def _():
            keys_ref[:, sl] = neg_key

        return 0

    lax.fori_loop(0, n_chunks * (chunk // ts), score_tile, 0)

    n_rb = tq // rb

    def count(pred, ref_val):
        ref_b = [jnp.broadcast_to(ref_val[r * rb:(r + 1) * rb, :], (rb, LANES)) for r in range(n_rb)]

        def body(c, accs):
            accs = list(accs)
            for u in range(chunk // LANES):
                sl = pl.ds(pl.multiple_of(c * chunk + u * LANES, LANES), LANES)
                for r in range(n_rb):
                    hit = pred(keys_ref[r * rb:(r + 1) * rb, sl], ref_b[r])
                    accs[r] = accs[r] + jnp.where(hit, 1, 0)
            return tuple(accs)

        accs = lax.fori_loop(0, n_chunks, body, tuple(jnp.zeros((rb, LANES), I32) for _ in range(n_rb)))
        return jnp.concatenate([jnp.sum(a, axis=1, keepdims=True) for a in accs], axis=0)

    def bisect(it, off):
        trial = off | jnp.left_shift(jnp.int32(1), 31 - it)
        cand = trial + jnp.int32(INT_MIN)
        return jnp.where(count(lambda kk, c: kk >= c, cand) >= topk, trial, off)

    thr = lax.fori_loop(0, 32, bisect, jnp.zeros((tq, 1), I32)) + jnp.int32(INT_MIN)
    need = (topk - count(lambda kk, c: kk > c, thr)).astype(F32)

    for hh in range(n_heads):
        qall_ref[hh * tq:(hh + 1) * tq, :] = q_ref[:, hh * HEAD_DIM:(hh + 1) * HEAD_DIM]
    m_ref[...] = jnp.full(m_ref.shape, NEG_INF, F32)
    acc_ref[...] = jnp.zeros(acc_ref.shape, F32)
    r2 = lax.broadcasted_iota(I32, (tk, tk), 0)
    c2 = lax.broadcasted_iota(I32, (tk, tk), 1)
    earlier = (r2 < c2).astype(BF16)
    ones_col = (lax.broadcasted_iota(I32, (tk, LANES), 1) == 0).astype(BF16)
    pos_q = posq_ref[...]
    n_hc = n_heads // hc
    rows_c = hc * tq
    n_tiles = (n_causal + tk - 1) // tk
    t_idx = i * tq + lax.broadcasted_iota(I32, (tq, tk), 0)
    col = lax.broadcasted_iota(I32, (tq, tk), 1)

    def attend_tile(j, eq_before):
        sl = pl.ds(pl.multiple_of(j * tk, tk), tk)
        kt = keys_ref[:, sl]
        eq = jnp.where(kt == thr, 1.0, 0.0)
        rank = eq_before + jnp.dot(eq.astype(BF16), earlier, preferred_element_type=F32)
        take = jnp.where(kt > thr, 1.0, jnp.where(rank < need, eq, 0.0))
        mask = jnp.where(j * tk + col <= t_idx, take, 0.0) > 0.5
        is_far = far_ref[(b * nq + i) * nk + j] == 1
        k_t = k_ref[sl, :]
        v_ext = jnp.concatenate([v_ref[sl, :], ones_col], axis=1)

        def near_bias(c):
            bucket = _t5_bucket(pos_q - posk_ref[0, :, sl])
            per_head = []
            for hh in range(c * hc, (c + 1) * hc):
                tbl = jnp.broadcast_to(rbt_ref[hh:hh + 1, :], (tq, LANES))
                halves = [jnp.take_along_axis(tbl, bucket[:, x * LANES:(x + 1) * LANES], axis=1,
                                              mode="promise_in_bounds") for x in range(tk // LANES)]
                per_head.append(jnp.concatenate(halves, axis=1))
            return jnp.stack(per_head, axis=0)

        for g in range(n_hc // par):
            chunks = range(par)
            rows = [slice((g * par + c) * rows_c, (g * par + c + 1) * rows_c) for c in chunks]
            s = [lax.dot_general(qall_ref[rows[c], :], k_t, _NT, preferred_element_type=F32).reshape(hc, tq, tk)
                 for c in chunks]
            s = [lax.cond(is_far,
                          lambda x=s[c]: jnp.where(mask[None], x, NEG_INF),
                          lambda x=s[c], c=c: jnp.where(mask[None], x + near_bias(g * par + c), NEG_INF)
                          ).reshape(rows_c, tk)
                 for c in chunks]
            shift = [jnp.where(is_far, rbfar_ref[rows[c], :], 0.0) for c in chunks]
            m_old = [m_ref[rows[c], :] for c in chunks]
            m_new = [jnp.maximum(m_old[c], jnp.max(s[c], axis=1, keepdims=True) + shift[c]) for c in chunks]
            p = [jnp.exp2(s[c] - (m_new[c] - shift[c])).astype(BF16) for c in chunks]
            pv = [jnp.dot(p[c], v_ext, preferred_element_type=F32) for c in chunks]
            for c in chunks:
                acc_ref[rows[c], :] = jnp.exp2(m_old[c] - m_new[c]) * acc_ref[rows[c], :] + pv[c]
                m_ref[rows[c], :] = m_new[c]
        return eq_before + jnp.sum(eq, axis=1, keepdims=True)

    lax.fori_loop(0, n_tiles, attend_tile, jnp.zeros((tq, 1), F32))
    for hh in range(n_heads):
        rows = slice(hh * tq, (hh + 1) * tq)
        o_ref[:, hh * HEAD_DIM:(hh + 1) * HEAD_DIM] = (
            acc_ref[rows, 0:HEAD_DIM] / acc_ref[rows, HEAD_DIM:HEAD_DIM + 1]).astype(o_ref.dtype)


def dsa_attention(qkv, idx, positions, rel_bias, batch, tq=256, tk=512, ts=256, chunk=1024, rb=128, hc=4,
                  par=2):
    t = qkv.shape[0]
    s = t // batch
    n_heads = (qkv.shape[1] - 2 * HEAD_DIM) // HEAD_DIM
    tq, tk, ts, chunk, rb = min(tq, s), min(tk, s), min(ts, s), min(chunk, s), min(rb, tq)
    assert chunk % tk == 0 and chunk % ts == 0 and s % chunk == 0 and tq % rb == 0
    hc = min(hc, n_heads)
    par = min(par, n_heads // hc)
    nq, nk = s // tq, s // tk
    topk = min(TOPK_MAX, s // 4)
    qi_w = N_IDX_HEADS * IDX_DIM
    pos_lo = positions.reshape(batch, nq, tq).min(axis=-1)
    pos_hi = positions.reshape(batch, nk, tk).max(axis=-1)
    far = (pos_lo[:, :, None] - pos_hi[:, None, :] >= MAX_DISTANCE).astype(I32).reshape(-1)
    rb2 = rel_bias.astype(F32) * LOG2E
    rbt = jnp.pad(rb2.T, ((0, 0), (0, LANES - N_BUCKETS)))
    rbfar = jnp.repeat(rb2[N_BUCKETS - 1], tq).reshape(n_heads * tq, 1)
    kernel = functools.partial(_dsa_kernel, tq=tq, tk=tk, ts=ts, chunk=chunk, rb=rb, nk=nk, topk=topk,
                               n_heads=n_heads, hc=hc, par=par)
    grid_spec = pltpu.PrefetchScalarGridSpec(
        num_scalar_prefetch=1,
        grid=(batch, nq),
        in_specs=[
            pl.BlockSpec((tq, n_heads * HEAD_DIM), lambda b, i, *_: (b * nq + i, 0)),
            pl.BlockSpec((s, HEAD_DIM), lambda b, i, *_: (b, n_heads)),
            pl.BlockSpec((s, HEAD_DIM), lambda b, i, *_: (b, n_heads + 1)),
            pl.BlockSpec((tq, qi_w), lambda b, i, *_: (b * nq + i, 0)),
            pl.BlockSpec((tq, LANES), lambda b, i, *_: (b * nq + i, qi_w // LANES)),
            pl.BlockSpec((s, LANES), lambda b, i, *_: (b, qi_w // LANES)),
            pl.BlockSpec((tq, 1), lambda b, i, *_: (b * nq + i, 0)),
            pl.BlockSpec((1, 1, s), lambda b, i, *_: (b, 0, 0)),
            pl.BlockSpec((n_heads, LANES), lambda b, i, *_: (0, 0)),
            pl.BlockSpec((n_heads * tq, 1), lambda b, i, *_: (0, 0)),
        ],
        out_specs=pl.BlockSpec((tq, n_heads * HEAD_DIM), lambda b, i, *_: (b * nq + i, 0)),
        scratch_shapes=[
            pltpu.VMEM((tq, s), I32),
            pltpu.VMEM((N_IDX_HEADS, tq, ts), F32),
            pltpu.VMEM((n_heads * tq, HEAD_DIM), BF16),
            pltpu.VMEM((n_heads * tq, 1), F32),
            pltpu.VMEM((n_heads * tq, 2 * HEAD_DIM), F32),
        ],
    )
    return pl.pallas_call(
        kernel,
        grid_spec=grid_spec,
        out_shape=jax.ShapeDtypeStruct((t, n_heads * HEAD_DIM), BF16),
        compiler_params=_params(("parallel", "arbitrary")),
        name="dsa_attention",
    )(far, qkv, qkv, qkv, idx, idx, idx, positions.reshape(t, 1), positions.reshape(batch, 1, s),
      rbt, rbfar)


def _mla_kernel(qn_ref, qr_ref, kn_ref, kr_ref, v_ref, o_ref, *, tq, tkb, group):
    i = pl.program_id(2)
    row = lax.broadcasted_iota(I32, (tq, tkb), 0)
    col = lax.broadcasted_iota(I32, (tq, tkb), 1)
    ones_col = (lax.broadcasted_iota(I32, (tkb, LANES), 1) == 0).astype(BF16)
    qs = [jnp.concatenate([qn_ref[:, h * LANES:(h + 1) * LANES], qr_ref[:, h * LANES:(h + 1) * LANES]], axis=1)
          for h in range(group)]
    n_full = (i * tq) // tkb

    def chunk(c, state, masked):
        sl = pl.ds(pl.multiple_of(c * tkb, tkb), tkb)
        kr = kr_ref[sl, :]
        heads = range(group)
        hs = [slice(h * LANES, (h + 1) * LANES) for h in heads]
        s = [lax.dot_general(qs[h], jnp.concatenate([kn_ref[sl, hs[h]], kr], axis=1), _NT,
                             preferred_element_type=F32) for h in heads]
        if masked:
            s = [jnp.where(c * tkb + col <= i * tq + row, x, NEG_INF) for x in s]
        m_new = [jnp.maximum(state[h][0], jnp.max(s[h], axis=1, keepdims=True)) for h in heads]
        p = [jnp.exp2(s[h] - m_new[h]).astype(BF16) for h in heads]
        pv = [jnp.dot(p[h], jnp.concatenate([v_ref[sl, hs[h]], ones_col], axis=1),
                      preferred_element_type=F32) for h in heads]
        return tuple((m_new[h], jnp.exp2(state[h][0] - m_new[h]) * state[h][1] + pv[h]) for h in heads)

    init = tuple((jnp.full((tq, 1), NEG_INF, F32), jnp.zeros((tq, 2 * LANES), F32)) for _ in range(group))
    state = lax.fori_loop(0, n_full, lambda c, st: chunk(c, st, False), init)
    state = chunk(n_full, state, True)
    for h in range(group):
        acc = state[h][1]
        o_ref[:, h * LANES:(h + 1) * LANES] = (acc[:, 0:V_DIM] / acc[:, V_DIM:V_DIM + 1]).astype(o_ref.dtype)


def mla_attention(qn, qr, kn, kr, v, batch, tq=256, tkb=512, group=4):
    t = qn.shape[0]
    s = t // batch
    h = qn.shape[1] // QK_NOPE
    tq, tkb, group = min(tq, s), min(tkb, s), min(group, h)
    nq = s // tq
    w = group * LANES
    qspec = pl.BlockSpec((tq, w), lambda b, hh, i: (b * nq + i, hh))
    kspec = pl.BlockSpec((s, w), lambda b, hh, i: (b, hh))
    return pl.pallas_call(
        functools.partial(_mla_kernel, tq=tq, tkb=tkb, group=group),
        grid=(batch, h // group, nq),
        in_specs=[qspec, qspec, kspec, pl.BlockSpec((s, LANES), lambda b, hh, i: (b, 0)), kspec],
        out_specs=qspec,
        out_shape=jax.ShapeDtypeStruct((t, h * V_DIM), BF16),
        compiler_params=_params(("parallel", "parallel", "arbitrary")),
        name="mla_attention",
    )(qn, qr, kn, kr, v)


def _first_max(vals):
    best = vals[0]
    for v in vals[1:]:
        best = jnp.maximum(best, v)
    idx = jnp.full(best.shape, len(vals) - 1, I32)
    for k in range(len(vals) - 2, -1, -1):
        idx = jnp.where(vals[k] == best, k, idx)
    return best, idx


def _router_kernel(x_ref, w_ref, b_ref, e_ref, wt_ref):
    logits = lax.dot_general(w_ref[...], x_ref[...], _NT, precision=lax.Precision.HIGHEST,
                             preferred_element_type=F32)
    aff = jax.nn.sigmoid(logits)
    biased = aff + b_ref[...]
    a_rows = [aff[e:e + 1, :] for e in range(N_EXPERTS)]
    b_rows = [biased[e:e + 1, :] for e in range(N_EXPERTS)]
    g_score, g_e1, g_e2 = [], [], []
    for g in range(N_GROUPS):
        vals = b_rows[g * EXPERTS_PER_GROUP:(g + 1) * EXPERTS_PER_GROUP]
        top1, i1 = _first_max(vals)
        rest = [jnp.where(i1 == k, -jnp.inf, vals[k]) for k in range(EXPERTS_PER_GROUP)]
        top2, i2 = _first_max(rest)
        g_score.append(top1 + top2)
        g_e1.append(i1 + g * EXPERTS_PER_GROUP)
        g_e2.append(i2 + g * EXPERTS_PER_GROUP)
    _, grp = _first_max(g_score)
    e1, e2 = g_e1[N_GROUPS - 1], g_e2[N_GROUPS - 1]
    for g in range(N_GROUPS - 2, -1, -1):
        e1 = jnp.where(grp == g, g_e1[g], e1)
        e2 = jnp.where(grp == g, g_e2[g], e2)
    w1 = jnp.zeros_like(a_rows[0])
    w2 = jnp.zeros_like(a_rows[0])
    for e in range(N_EXPERTS):
        w1 = jnp.where(e1 == e, a_rows[e], w1)
        w2 = jnp.where(e2 == e, a_rows[e], w2)
    tot = w1 + w2
    e_ref[0:1, :] = e1
    e_ref[1:2, :] = e2
    wt_ref[0:1, :] = w1 / tot
    wt_ref[1:2, :] = w2 / tot


def router(x, w_router, router_bias, tm=512):
    t, d = x.shape
    tm = min(tm, t)
    return pl.pallas_call(
        _router_kernel,
        grid=(t // tm,),
        in_specs=[pl.BlockSpec((tm, d), lambda i: (i, 0)),
                  pl.BlockSpec((N_EXPERTS, d), lambda i: (0, 0)),
                  pl.BlockSpec((N_EXPERTS, 1), lambda i: (0, 0))],
        out_specs=[pl.BlockSpec((TOP_K_EXPERTS, tm), lambda i: (0, i)),
                   pl.BlockSpec((TOP_K_EXPERTS, tm), lambda i: (0, i))],
        out_shape=[jax.ShapeDtypeStruct((TOP_K_EXPERTS, t), I32),
                   jax.ShapeDtypeStruct((TOP_K_EXPERTS, t), F32)],
        compiler_params=_params(("parallel",)),
        name="router",
    )(x, w_router.T, router_bias.reshape(N_EXPERTS, 1))


def dispatch_plan(experts, tm):
    k, t = experts.shape
    n_slots = k * t
    n_rows = n_slots + N_EXPERTS * tm
    n_tiles = n_rows // tm
    flat = experts.reshape(n_slots)
    order = jnp.argsort(flat, stable=True).astype(I32)
    counts = jnp.zeros((N_EXPERTS,), I32).at[flat].add(1)
    padded = (counts + tm - 1) // tm * tm
    pad_end = jnp.cumsum(padded)
    start = jnp.cumsum(counts) - counts
    e_sorted = flat[order]
    dest_sorted = (pad_end - padded)[e_sorted] + jnp.arange(n_slots, dtype=I32) - start[e_sorted]
    row_token = jnp.zeros((n_rows,), I32).at[dest_sorted].set(order % t)
    dest = jnp.zeros((n_slots,), I32).at[order].set(dest_sorted)
    tile_start = jnp.arange(n_tiles, dtype=I32) * tm
    tile_expert = jnp.minimum(jnp.searchsorted(pad_end, tile_start, side="right"), N_EXPERTS - 1).astype(I32)
    tile_active = (tile_start < pad_end[-1]).astype(I32)
    return row_token.reshape(n_tiles, 1, tm), dest.reshape(k, t), tile_expert, tile_active


def _gather_kernel(act_ref, idx_ref, x_hbm, o_ref, buf, sem, *, tm):
    t = pl.program_id(0)

    def row_copy(r):
        return pltpu.make_async_copy(x_hbm.at[pl.ds(idx_ref[0, 0, r], 1), :], buf.at[pl.ds(r, 1), :], sem)

    @pl.when(act_ref[t] == 1)
    def _():
        def start(r, c):
            row_copy(r).start()
            return c

        def wait(r, c):
            row_copy(r).wait()
            return c

        lax.fori_loop(0, tm, start, 0)
        lax.fori_loop(0, tm, wait, 0)
        o_ref[...] = buf[...].astype(o_ref.dtype)

    @pl.when(act_ref[t] == 0)
    def _():
        o_ref[...] = jnp.zeros_like(o_ref)


def gather_rows(x, row_token, tile_active):
    n_tiles, _, tm = row_token.shape
    d = x.shape[1]
    grid_spec = pltpu.PrefetchScalarGridSpec(
        num_scalar_prefetch=1,
        grid=(n_tiles,),
        in_specs=[pl.BlockSpec((1, 1, tm), lambda t, *_: (t, 0, 0), memory_space=pltpu.SMEM),
                  pl.BlockSpec(memory_space=pl.ANY)],
        out_specs=pl.BlockSpec((tm, d), lambda t, *_: (t, 0)),
        scratch_shapes=[pltpu.VMEM((tm, d), F32), pltpu.SemaphoreType.DMA(())],
    )
    return pl.pallas_call(
        functools.partial(_gather_kernel, tm=tm),
        grid_spec=grid_spec,
        out_shape=jax.ShapeDtypeStruct((n_tiles * tm, d), BF16),
        compiler_params=_params(("arbitrary",)),
        name="gather_rows",
    )(tile_active, row_token, x)


def _ffn_kernel(exp_ref, act_ref, x_ref, wg_ref, wu_ref, wd_ref, o_ref):
    t = pl.program_id(0)

    @pl.when(act_ref[t] == 1)
    def _():
        x = x_ref[...]
        h = jax.nn.silu(jnp.dot(x, wg_ref[0], preferred_element_type=F32)) * jnp.dot(
            x, wu_ref[0], preferred_element_type=F32)
        o_ref[...] = jnp.dot(h.astype(BF16), wd_ref[0], preferred_element_type=F32)

    @pl.when(act_ref[t] == 0)
    def _():
        o_ref[...] = jnp.zeros_like(o_ref)


def expert_ffn(xs, tile_expert, tile_active, w_gate, w_up, w_down, tm):
    n, d = xs.shape
    f = w_gate.shape[2]
    grid_spec = pltpu.PrefetchScalarGridSpec(
        num_scalar_prefetch=2,
        grid=(n // tm,),
        in_specs=[pl.BlockSpec((tm, d), lambda t, ex, ac: (t, 0)),
                  pl.BlockSpec((1, d, f), lambda t, ex, ac: (ex[t], 0, 0)),
                  pl.BlockSpec((1, d, f), lambda t, ex, ac: (ex[t], 0, 0)),
                  pl.BlockSpec((1, f, d), lambda t, ex, ac: (ex[t], 0, 0))],
        out_specs=pl.BlockSpec((tm, d), lambda t, ex, ac: (t, 0)),
    )
    return pl.pallas_call(
        _ffn_kernel,
        grid_spec=grid_spec,
        out_shape=jax.ShapeDtypeStruct((n, d), F32),
        compiler_params=_params(("arbitrary",)),
        name="expert_ffn",
    )(tile_expert, tile_active, xs, w_gate, w_up, w_down)


def _combine_ln_kernel(dest_ref, x_ref, w_ref, g_ref, b_ref, y_hbm, o_ref, ob_ref, buf, sem, *, tm):
    def row_copy(k, r):
        return pltpu.make_async_copy(y_hbm.at[pl.ds(dest_ref[0, k, r], 1), :], buf.at[k, pl.ds(r, 1), :], sem)

    def start(r, c):
        for k in range(TOP_K_EXPERTS):
            row_copy(k, r).start()
        return c

    def wait(r, c):
        for k in range(TOP_K_EXPERTS):
            row_copy(k, r).wait()
        return c

    lax.fori_loop(0, tm, start, 0)
    lax.fori_loop(0, tm, wait, 0)
    h = DEEPNORM_ALPHA * x_ref[...] + (w_ref[:, 0:1] * buf[0] + w_ref[:, 1:2] * buf[1])
    mu = jnp.mean(h, axis=-1, keepdims=True)
    dlt = h - mu
    var = jnp.mean(dlt * dlt, axis=-1, keepdims=True)
    y = dlt * lax.rsqrt(var + LN_EPS) * g_ref[...] + b_ref[...]
    o_ref[...] = y
    ob_ref[...] = y.astype(BF16)


def combine_ln(x, ys, dest, weights, g, b, tm=256):
    t, d = x.shape
    tm = min(tm, t)
    k = dest.shape[0]
    dest_tiles = dest.reshape(k, t // tm, tm).transpose(1, 0, 2)
    row = pl.BlockSpec((tm, d), lambda i: (i, 0))
    vec = pl.BlockSpec((1, d), lambda i: (0, 0))
    return pl.pallas_call(
        functools.partial(_combine_ln_kernel, tm=tm),
        grid=(t // tm,),
        in_specs=[pl.BlockSpec((1, k, tm), lambda i: (i, 0, 0), memory_space=pltpu.SMEM),
                  row,
                  pl.BlockSpec((tm, k), lambda i: (i, 0)),
                  vec, vec,
                  pl.BlockSpec(memory_space=pl.ANY)],
        out_specs=[row, row],
        out_shape=[jax.ShapeDtypeStruct((t, d), F32), jax.ShapeDtypeStruct((t, d), BF16)],
        scratch_shapes=[pltpu.VMEM((k, tm, d), F32), pltpu.SemaphoreType.DMA(())],
        compiler_params=_params(("arbitrary",)),
        name="combine_ln",
    )(dest_tiles, x, weights.T, g.reshape(1, d), b.reshape(1, d), ys)


def even_mixer(xb, positions, w_in, w_out, rel_bias, batch):
    q_scale = HEAD_DIM ** -0.5 * LOG2E
    o0, o1, o2, o3, o4, o5, o6, o7 = (SB_W, 2 * SB_W, 3 * SB_W, 3 * SB_W + DSA_W, 3 * SB_W + DSA_W + HEAD_DIM,
                                      3 * SB_W + DSA_W + 2 * HEAD_DIM,
                                      3 * SB_W + DSA_W + 2 * HEAD_DIM + N_IDX_HEADS * IDX_DIM,
                                      3 * SB_W + DSA_W + 2 * HEAD_DIM + N_IDX_HEADS * IDX_DIM + N_IDX_HEADS)
    w_sb = jnp.concatenate([w_in[:, :o0] * q_scale, w_in[:, o0:o2]], axis=1).astype(BF16)
    w_ds = jnp.concatenate([w_in[:, o2:o3] * q_scale, w_in[:, o3:o5]], axis=1).astype(BF16)
    pad = jnp.zeros((w_in.shape[0], LANES - IDX_DIM - N_IDX_HEADS), w_in.dtype)
    w_ix = jnp.concatenate([w_in[:, o5:o6] * IDX_DIM ** -0.5, w_in[:, o7:], w_in[:, o6:o7], pad],
                           axis=1).astype(BF16)
    qkv_sb = matmul(xb, w_sb, BF16, 1024, 512)
    qkv_ds = matmul(xb, w_ds, BF16, 1024, 768)
    idx = matmul(xb, w_ix, F32, 1024, w_ix.shape[1])
    o_sb = sb_attention(qkv_sb, batch)
    o_ds = dsa_attention(qkv_ds, idx, positions, rel_bias, batch)
    w_out = w_out.astype(BF16)
    return matmul2(o_sb, o_ds, w_out[:SB_W], w_out[SB_W:], F32, 1024, 512)


def mla_mixer(xb, positions, w_down, g_q, g_kv, w_uq, w_ukv, w_o, batch):
    t = xb.shape[0]
    scale = (QK_NOPE + QK_ROPE) ** -0.5 * LOG2E
    w_down = w_down.astype(BF16)
    kr_pad = jnp.zeros((w_down.shape[0], LANES - QK_ROPE), BF16)
    c_q = matmul(xb, w_down[:, :Q_LORA], F32, 1024, 512)
    c_kv = matmul(xb, w_down[:, Q_LORA:Q_LORA + KV_LORA], F32, 1024, 512)
    k_rope = matmul(xb, jnp.concatenate([w_down[:, Q_LORA + KV_LORA:], kr_pad], axis=1), F32, 1024, LANES)
    c_q = rms_norm(c_q, g_q)
    c_kv = rms_norm(c_kv, g_kv)
    w_uq = (w_uq * scale).reshape(Q_LORA, N_HEADS_MLA, QK_NOPE + QK_ROPE)
    w_qn = w_uq[:, :, :QK_NOPE].reshape(Q_LORA, N_HEADS_MLA * QK_NOPE).astype(BF16)
    w_qr = jnp.pad(w_uq[:, :, QK_NOPE:], ((0, 0), (0, 0), (0, LANES - QK_ROPE)))
    w_qr = w_qr.reshape(Q_LORA, N_HEADS_MLA * LANES).astype(BF16)
    w_ukv = w_ukv.reshape(KV_LORA, N_HEADS_MLA, QK_NOPE + V_DIM)
    w_kn = w_ukv[:, :, :QK_NOPE].reshape(KV_LORA, N_HEADS_MLA * QK_NOPE).astype(BF16)
    w_v = w_ukv[:, :, QK_NOPE:].reshape(KV_LORA, N_HEADS_MLA * V_DIM).astype(BF16)
    qn = matmul(c_q, w_qn, BF16, 1024, 1024)
    qr = matmul(c_q, w_qr, F32, 1024, 1024)
    kn = matmul(c_kv, w_kn, BF16, 1024, 1024)
    v = matmul(c_kv, w_v, BF16, 1024, 1024)
    half = QK_ROPE // 2
    inv_freq = ROPE_THETA ** (-jnp.arange(half, dtype=F32) / half)
    inv_freq = jnp.tile(inv_freq, LANES // half).reshape(1, LANES)
    pos_col = positions.reshape(t, 1)
    qr = rope(qr, pos_col, inv_freq)
    kr = rope(k_rope, pos_col, inv_freq)
    o = mla_attention(qn, qr, kn, kr, v, batch)
    return matmul(o, w_o.astype(BF16), F32, 1024, 512)


def moe_ln(x, w_router, router_bias, w_gate, w_up, w_down, g, b, tm=256):
    experts, weights = router(x, w_router, router_bias)
    row_token, dest, tile_expert, tile_active = dispatch_plan(experts, tm)
    xs = gather_rows(x, row_token, tile_active)
    ys = expert_ffn(xs, tile_expert, tile_active, w_gate.astype(BF16), w_up.astype(BF16),
                    w_down.astype(BF16), tm)
    return combine_ln(x, ys, dest, weights, g, b)


def kernel(x, positions, rel_bias, even_w_in, even_w_out, mla_w_down, mla_g_q, mla_g_kv, mla_w_uq,
           mla_w_ukv, mla_w_o, w_router, router_bias, exp_w_gate, exp_w_up, exp_w_down, ln_g, ln_b):
    batch, seq, d = x.shape
    x = x.reshape(batch * seq, d)
    xb = x.astype(BF16)
    for layer in range(DEPTH):
        i = layer // 2
        if layer % 2 == 0:
            mix = even_mixer(xb, positions, even_w_in[i], even_w_out[i], rel_bias, batch)
        else:
            mix = mla_mixer(xb, positions, mla_w_down[i], mla_g_q[i], mla_g_kv[i], mla_w_uq[i],
                            mla_w_ukv[i], mla_w_o[i], batch)
        x, xb = deepnorm_ln(x, mix, ln_g[layer, 0], ln_b[layer, 0])
        x, xb = moe_ln(x, w_router, router_bias, exp_w_gate[layer], exp_w_up[layer], exp_w_down[layer],
                       ln_g[layer, 1], ln_b[layer, 1])
    return x.reshape(batch, seq, d)
```

```python
import functools
import math

import jax
import jax.numpy as jnp
from jax import lax
from jax.experimental import pallas as pl
from jax.experimental.pallas import tpu as pltpu

F32 = jnp.float32
BF16 = jnp.bfloat16
I32 = jnp.int32

D_MODEL = 4096
DEPTH = 4
HEAD_DIM = 128
N_HEADS_SB = 16
N_HEADS_DSA = 16
N_IDX_HEADS = 8
IDX_DIM = 64
TOPK_MAX = 256
N_BUCKETS = 32
MAX_DISTANCE = 128
N_HEADS_MLA = 32
Q_LORA = 1024
KV_LORA = 512
QK_NOPE = 128
QK_ROPE = 64
V_DIM = 128
ROPE_THETA = 10000.0
N_EXPERTS = 16
N_GROUPS = 4
EXPERTS_PER_GROUP = N_EXPERTS // N_GROUPS
TOP_K_EXPERTS = 2
D_FF_EXPERT = 768
DEEPNORM_ALPHA = (2 * DEPTH) ** 0.25
LN_EPS = 1e-5
RMS_EPS = 1e-6
NEG_INF = -1e30
LOG2E = math.log2(math.e)
SB_W = N_HEADS_SB * HEAD_DIM
DSA_W = N_HEADS_DSA * HEAD_DIM

LANES = 128
INT_MIN = -(2 ** 31)
VMEM_LIMIT = 56 * 1024 * 1024

_NT = (((1,), (1,)), ((), ()))


def _params(sem, vmem=VMEM_LIMIT):
    return pltpu.CompilerParams(dimension_semantics=sem, vmem_limit_bytes=vmem)


def _mm_kernel(a_ref, b_ref, o_ref):
    o_ref[...] = jnp.dot(a_ref[...], b_ref[...], preferred_element_type=F32).astype(o_ref.dtype)


def _mm2_kernel(a1_ref, a2_ref, b1_ref, b2_ref, o_ref):
    acc = jnp.dot(a1_ref[...], b1_ref[...], preferred_element_type=F32)
    acc = acc + jnp.dot(a2_ref[...], b2_ref[...], preferred_element_type=F32)
    o_ref[...] = acc.astype(o_ref.dtype)


def matmul(a, b, out_dtype, tm, tn):
    m, k = a.shape
    n = b.shape[1]
    tm, tn = min(tm, m), min(tn, n)
    assert m % tm == 0 and n % tn == 0
    return pl.pallas_call(
        _mm_kernel,
        grid=(m // tm, n // tn),
        in_specs=[pl.BlockSpec((tm, k), lambda i, j: (i, 0)),
                  pl.BlockSpec((k, tn), lambda i, j: (0, j))],
        out_specs=pl.BlockSpec((tm, tn), lambda i, j: (i, j)),
        out_shape=jax.ShapeDtypeStruct((m, n), out_dtype),
        compiler_params=_params(("parallel", "arbitrary")),
        name="matmul",
    )(a, b)


def matmul2(a1, a2, b1, b2, out_dtype, tm, tn):
    m, k1 = a1.shape
    k2 = a2.shape[1]
    n = b1.shape[1]
    tm, tn = min(tm, m), min(tn, n)
    assert m % tm == 0 and n % tn == 0
    return pl.pallas_call(
        _mm2_kernel,
        grid=(m // tm, n // tn),
        in_specs=[pl.BlockSpec((tm, k1), lambda i, j: (i, 0)),
                  pl.BlockSpec((tm, k2), lambda i, j: (i, 0)),
                  pl.BlockSpec((k1, tn), lambda i, j: (0, j)),
                  pl.BlockSpec((k2, tn), lambda i, j: (0, j))],
        out_specs=pl.BlockSpec((tm, tn), lambda i, j: (i, j)),
        out_shape=jax.ShapeDtypeStruct((m, n), out_dtype),
        compiler_params=_params(("parallel", "arbitrary")),
        name="matmul2",
    )(a1, a2, b1, b2)


def _ln_kernel(x_ref, m_ref, g_ref, b_ref, o_ref, ob_ref):
    h = DEEPNORM_ALPHA * x_ref[...] + m_ref[...]
    mu = jnp.mean(h, axis=-1, keepdims=True)
    d = h - mu
    var = jnp.mean(d * d, axis=-1, keepdims=True)
    y = d * lax.rsqrt(var + LN_EPS) * g_ref[...] + b_ref[...]
    o_ref[...] = y
    ob_ref[...] = y.astype(BF16)


def deepnorm_ln(x, mix, g, b, tm=128):
    t, d = x.shape
    tm = min(tm, t)
    row = pl.BlockSpec((tm, d), lambda i: (i, 0))
    vec = pl.BlockSpec((1, d), lambda i: (0, 0))
    return pl.pallas_call(
        _ln_kernel,
        grid=(t // tm,),
        in_specs=[row, row, vec, vec],
        out_specs=[row, row],
        out_shape=[jax.ShapeDtypeStruct((t, d), F32), jax.ShapeDtypeStruct((t, d), BF16)],
        compiler_params=_params(("parallel",)),
        name="deepnorm_ln",
    )(x, mix, g.reshape(1, d), b.reshape(1, d))


def _rms_kernel(x_ref, g_ref, o_ref):
    x = x_ref[...]
    y = x * lax.rsqrt(jnp.mean(x * x, axis=-1, keepdims=True) + RMS_EPS) * g_ref[...]
    o_ref[...] = y.astype(o_ref.dtype)


def rms_norm(x, g, tm=256):
    t, d = x.shape
    tm = min(tm, t)
    return pl.pallas_call(
        _rms_kernel,
        grid=(t // tm,),
        in_specs=[pl.BlockSpec((tm, d), lambda i: (i, 0)), pl.BlockSpec((1, d), lambda i: (0, 0))],
        out_specs=pl.BlockSpec((tm, d), lambda i: (i, 0)),
        out_shape=jax.ShapeDtypeStruct((t, d), BF16),
        compiler_params=_params(("parallel",)),
        name="rms_norm",
    )(x, g.reshape(1, d))


def _rope_kernel(x_ref, pos_ref, f_ref, o_ref):
    lane = lax.broadcasted_iota(I32, (1, LANES), 1)
    first = (lane % QK_ROPE) < (QK_ROPE // 2)
    ang = pos_ref[...].astype(F32) * f_ref[...]
    cos = jnp.cos(ang)
    sin = jnp.sin(ang)
    sin = jnp.where(first, -sin, sin)
    for c in range(x_ref.shape[1] // LANES):
        x = x_ref[:, c * LANES:(c + 1) * LANES]
        partner = jnp.where(first, pltpu.roll(x, LANES - QK_ROPE // 2, 1), pltpu.roll(x, QK_ROPE // 2, 1))
        o_ref[:, c * LANES:(c + 1) * LANES] = (x * cos + partner * sin).astype(o_ref.dtype)


def rope(x, pos_col, inv_freq_lanes, tm=256):
    t, w = x.shape
    tm = min(tm, t)
    return pl.pallas_call(
        _rope_kernel,
        grid=(t // tm,),
        in_specs=[pl.BlockSpec((tm, w), lambda i: (i, 0)),
                  pl.BlockSpec((tm, 1), lambda i: (i, 0)),
                  pl.BlockSpec((1, LANES), lambda i: (0, 0))],
        out_specs=pl.BlockSpec((tm, w), lambda i: (i, 0)),
        out_shape=jax.ShapeDtypeStruct((t, w), BF16),
        compiler_params=_params(("parallel",)),
        name="rope",
    )(x, pos_col, inv_freq_lanes)


def _sb_kernel(q_ref, k_ref, v_ref, o_ref, *, tile, group):
    i = pl.program_id(2)
    row = lax.broadcasted_iota(I32, (tile, tile), 0)
    col = lax.broadcasted_iota(I32, (tile, tile), 1)
    later = (row > col).astype(BF16)
    strict = col < row
    qs = [q_ref[:, h * HEAD_DIM:(h + 1) * HEAD_DIM] for h in range(group)]

    def step(j, state, diag):
        sl = pl.ds(pl.multiple_of(j * tile, tile), tile)
        heads = range(group)
        hs = [slice(h * HEAD_DIM, (h + 1) * HEAD_DIM) for h in heads]
        z = [lax.dot_general(qs[h], k_ref[sl, hs[h]], _NT, preferred_element_type=F32) for h in heads]
        u = [jnp.log2(1.0 + jnp.exp2(-jnp.abs(z[h]))) for h in heads]
        drop = [jnp.maximum(z[h], 0.0) + u[h] for h in heads]
        if diag:
            drop = [jnp.where(strict, d, 0.0) for d in drop]
        after = [jnp.dot(drop[h].astype(BF16), later, preferred_element_type=F32) for h in heads]
        w = [jnp.exp2(jnp.minimum(z[h], 0.0) - u[h] - after[h] - state[h][0]) for h in heads]
        if diag:
            w = [jnp.where(strict, x, 0.0) for x in w]
        pv = [jnp.dot(w[h].astype(BF16), v_ref[sl, hs[h]], preferred_element_type=F32) for h in heads]
        return tuple((state[h][0] + (after[h][:, 0:1] + drop[h][:, 0:1]), state[h][1] + pv[h]) for h in heads)

    init = tuple((jnp.zeros((tile, 1), F32), jnp.zeros((tile, HEAD_DIM), F32)) for _ in range(group))
    state = step(i, init, True)
    state = lax.fori_loop(0, i, lambda n, st: step(i - 1 - n, st, False), state)
    for h in range(group):
        o_ref[:, h * HEAD_DIM:(h + 1) * HEAD_DIM] = state[h][1].astype(o_ref.dtype)


def sb_attention(qkv, batch, tile=256, group=4):
    t = qkv.shape[0]
    s = t // batch
    h = qkv.shape[1] // (3 * HEAD_DIM)
    tile, group = min(tile, s), min(group, h)
    nq = s // tile
    ng = h // group
    w = group * HEAD_DIM
    return pl.pallas_call(
        functools.partial(_sb_kernel, tile=tile, group=group),
        grid=(batch, ng, nq),
        in_specs=[pl.BlockSpec((tile, w), lambda b, hh, i: (b * nq + i, hh)),
                  pl.BlockSpec((s, w), lambda b, hh, i: (b, ng + hh)),
                  pl.BlockSpec((s, w), lambda b, hh, i: (b, 2 * ng + hh))],
        out_specs=pl.BlockSpec((tile, w), lambda b, hh, i: (b * nq + i, hh)),
        out_shape=jax.ShapeDtypeStruct((t, h * HEAD_DIM), BF16),
        compiler_params=_params(("parallel", "parallel", "arbitrary")),
        name="sb_attention",
    )(qkv, qkv, qkv)


def _t5_bucket(dist):
    n = jnp.maximum(dist, 0)
    max_exact = N_BUCKETS // 2
    scaled = jnp.log(jnp.maximum(n, 1).astype(F32) / max_exact) / math.log(MAX_DISTANCE / max_exact)
    large = jnp.minimum(max_exact + (scaled * (N_BUCKETS - max_exact)).astype(I32), N_BUCKETS - 1)
    return jnp.where(n < max_exact, n, large)


def _dsa_kernel(far_ref,
                q_ref, k_ref, v_ref, qi_ref, wk_q_ref, wk_k_ref, posq_ref, posk_ref, rbt_ref, rbfar_ref,
                o_ref,
                keys_ref, wb_ref, qall_ref, m_ref, acc_ref, s_ref,
                *, tq, tk, ts, chunk, rb, nk, topk, n_heads, hc, par):
    b = pl.program_id(0)
    i = pl.program_id(1)
    nq = pl.num_programs(1)
    n_causal = (i + 1) * tq
    n_chunks = (n_causal + chunk - 1) // chunk

    qi = qi_ref[...].astype(BF16)
    for hh in range(N_IDX_HEADS):
        wb_ref[hh] = jnp.broadcast_to(wk_q_ref[:, IDX_DIM + hh:IDX_DIM + hh + 1], (tq, ts))

    def to_key(sc):
        sc = jnp.where(sc == 0.0, 0.0, sc)
        bits = pltpu.bitcast(sc, I32)
        return bits ^ ((bits >> 31) & 0x7FFFFFFF)

    neg_key = to_key(jnp.full((tq, ts), NEG_INF, F32))
    t_idx_s = i * tq + lax.broadcasted_iota(I32, (tq, ts), 0)
    col_s = lax.broadcasted_iota(I32, (tq, ts), 1)

    def score_tile(j, _):
        sl = pl.ds(pl.multiple_of(j * ts, ts), ts)

        @pl.when(j * ts < n_causal)
        def _():
            kj = wk_k_ref[sl, 0:IDX_DIM].astype(BF16)
            sc = jnp.zeros((tq, ts), F32)
            for hh in range(N_IDX_HEADS):
                lg = lax.dot_general(qi[:, hh * IDX_DIM:(hh + 1) * IDX_DIM], kj, _NT,
                                     preferred_element_type=F32)
                sc = sc + wb_ref[hh] * jnp.maximum(lg, 0.0)
            sc = sc * (N_IDX_HEADS ** -0.5)
            sc = jnp.where(j * ts + col_s <= t_idx_s, sc, NEG_INF)
            keys_ref[:, sl] = to_key(sc)

        @pl.when(j * ts >= n_causal)
        def _():
            keys_ref[:, sl] = neg_key

        return 0

    lax.fori_loop(0, n_chunks * (chunk // ts), score_tile, 0)

    n_rb = tq // rb

    def count(pred, ref_val):
        ref_b = [jnp.broadcast_to(ref_val[r * rb:(r + 1) * rb, :], (rb, LANES)) for r in range(n_rb)]

        def body(c, accs):
            accs = list(accs)
            for u in range(chunk // LANES):
                sl = pl.ds(pl.multiple_of(c * chunk + u * LANES, LANES), LANES)
                for r in range(n_rb):
                    hit = pred(keys_ref[r * rb:(r + 1) * rb, sl], ref_b[r])
                    accs[r] = accs[r] + jnp.where(hit, 1, 0)
            return tuple(accs)

        accs = lax.fori_loop(0, n_chunks, body, tuple(jnp.zeros((rb, LANES), I32) for _ in range(n_rb)))
        return jnp.concatenate([jnp.sum(a, axis=1, keepdims=True) for a in accs], axis=0)

    def bisect(it, off):
        trial = off | jnp.left_shift(jnp.int32(1), 31 - it)
        cand = trial + jnp.int32(INT_MIN)
        return jnp.where(count(lambda kk, c: kk >= c, cand) >= topk, trial, off)

    thr = lax.fori_loop(0, 32, bisect, jnp.zeros((tq, 1), I32)) + jnp.int32(INT_MIN)
    need = (topk - count(lambda kk, c: kk > c, thr)).astype(F32)

    for hh in range(n_heads):
        qall_ref[hh * tq:(hh + 1) * tq, :] = q_ref[:, hh * HEAD_DIM:(hh + 1) * HEAD_DIM]
    m_ref[...] = jnp.full(m_ref.shape, NEG_INF, F32)
    acc_ref[...] = jnp.zeros(acc_ref.shape, F32)
    r2 = lax.broadcasted_iota(I32, (tk, tk), 0)
    c2 = lax.broadcasted_iota(I32, (tk, tk), 1)
    earlier = (r2 < c2).astype(BF16)
    ones_col = (lax.broadcasted_iota(I32, (tk, LANES), 1) == 0).astype(BF16)
    pos_q = posq_ref[...]
    n_hc = n_heads // hc
    rows_c = hc * tq
    n_tiles = (n_causal + tk - 1) // tk
    t_idx = i * tq + lax.broadcasted_iota(I32, (tq, tk), 0)
    col = lax.broadcasted_iota(I32, (tq, tk), 1)

    def attend_tile(j, eq_before):
        sl = pl.ds(pl.multiple_of(j * tk, tk), tk)
        kt = keys_ref[:, sl]
        eq = jnp.where(kt == thr, 1.0, 0.0)
        rank = eq_before + jnp.dot(eq.astype(BF16), earlier, preferred_element_type=F32)
        take = jnp.where(kt > thr, 1.0, jnp.where(rank < need, eq, 0.0))
        mask = jnp.where(j * tk + col <= t_idx, take, 0.0) > 0.5
        is_far = far_ref[(b * nq + i) * nk + j] == 1
        k_t = k_ref[sl, :]
        v_ext = jnp.concatenate([v_ref[sl, :], ones_col], axis=1)

        def near_bias(c):
            bucket = _t5_bucket(pos_q - posk_ref[0, :, sl])
            per_head = []
            for hh in range(c * hc, (c + 1) * hc):
                tbl = jnp.broadcast_to(rbt_ref[hh:hh + 1, :], (tq, LANES))
                halves = [jnp.take_along_axis(tbl, bucket[:, x * LANES:(x + 1) * LANES], axis=1,
                                              mode="promise_in_bounds") for x in range(tk // LANES)]
                per_head.append(jnp.concatenate(halves, axis=1))
            return jnp.stack(per_head, axis=0)

        for g in range(n_hc // par):
            chunks = range(par)
            rows = [slice((g * par + c) * rows_c, (g * par + c + 1) * rows_c) for c in chunks]
            local = [slice(c * rows_c, (c + 1) * rows_c) for c in chunks]
            for c in chunks:
                s_ref[local[c], :] = lax.dot_general(qall_ref[rows[c], :], k_t, _NT, preferred_element_type=F32)

            @pl.when(jnp.logical_not(is_far))
            def _():
                for c in chunks:
                    s_ref[local[c], :] += near_bias(g * par + c).reshape(rows_c, tk)

            s = [jnp.where(mask[None], s_ref[local[c], :].reshape(hc, tq, tk), NEG_INF).reshape(rows_c, tk)
                 for c in chunks]
            shift = [jnp.where(is_far, rbfar_ref[rows[c], :], 0.0) for c in chunks]
            m_old = [m_ref[rows[c], :] for c in chunks]
            m_new = [jnp.maximum(m_old[c], jnp.max(s[c], axis=1, keepdims=True) + shift[c]) for c in chunks]
            p = [jnp.exp2(s[c] - (m_new[c] - shift[c])).astype(BF16) for c in chunks]
            pv = [jnp.dot(p[c], v_ext, preferred_element_type=F32) for c in chunks]
            for c in chunks:
                acc_ref[rows[c], :] = jnp.exp2(m_old[c] - m_new[c]) * acc_ref[rows[c], :] + pv[c]
                m_ref[rows[c], :] = m_new[c]
        return eq_before + jnp.sum(eq, axis=1, keepdims=True)

    lax.fori_loop(0, n_tiles, attend_tile, jnp.zeros((tq, 1), F32))
    for hh in range(n_heads):
        rows = slice(hh * tq, (hh + 1) * tq)
        o_ref[:, hh * HEAD_DIM:(hh + 1) * HEAD_DIM] = (
            acc_ref[rows, 0:HEAD_DIM] / acc_ref[rows, HEAD_DIM:HEAD_DIM + 1]).astype(o_ref.dtype)


def dsa_attention(qkv, idx, positions, rel_bias, batch, tq=256, tk=512, ts=256, chunk=1024, rb=128, hc=4,
                  par=2):
    t = qkv.shape[0]
    s = t // batch
    n_heads = (qkv.shape[1] - 2 * HEAD_DIM) // HEAD_DIM
    tq, tk, ts, chunk, rb = min(tq, s), min(tk, s), min(ts, s), min(chunk, s), min(rb, tq)
    assert chunk % tk == 0 and chunk % ts == 0 and s % chunk == 0 and tq % rb == 0
    hc = min(hc, n_heads)
    par = min(par, n_heads // hc)
    nq, nk = s // tq, s // tk
    topk = min(TOPK_MAX, s // 4)
    qi_w = N_IDX_HEADS * IDX_DIM
    pos_lo = positions.reshape(batch, nq, tq).min(axis=-1)
    pos_hi = positions.reshape(batch, nk, tk).max(axis=-1)
    far = (pos_lo[:, :, None] - pos_hi[:, None, :] >= MAX_DISTANCE).astype(I32).reshape(-1)
    rb2 = rel_bias.astype(F32) * LOG2E
    rbt = jnp.pad(rb2.T, ((0, 0), (0, LANES - N_BUCKETS)))
    rbfar = jnp.repeat(rb2[N_BUCKETS - 1], tq).reshape(n_heads * tq, 1)
    kernel = functools.partial(_dsa_kernel, tq=tq, tk=tk, ts=ts, chunk=chunk, rb=rb, nk=nk, topk=topk,
                               n_heads=n_heads, hc=hc, par=par)
    once = pl.Buffered(1)
    grid_spec = pltpu.PrefetchScalarGridSpec(
        num_scalar_prefetch=1,
        grid=(batch, nq),
        in_specs=[
            pl.BlockSpec((tq, n_heads * HEAD_DIM), lambda b, i, *_: (b * nq + i, 0)),
            pl.BlockSpec((s, HEAD_DIM), lambda b, i, *_: (b, n_heads), pipeline_mode=once),
            pl.BlockSpec((s, HEAD_DIM), lambda b, i, *_: (b, n_heads + 1), pipeline_mode=once),
            pl.BlockSpec((tq, qi_w), lambda b, i, *_: (b * nq + i, 0)),
            pl.BlockSpec((tq, LANES), lambda b, i, *_: (b * nq + i, qi_w // LANES)),
            pl.BlockSpec((s, LANES), lambda b, i, *_: (b, qi_w // LANES), pipeline_mode=once),
            pl.BlockSpec((tq, 1), lambda b, i, *_: (b * nq + i, 0)),
            pl.BlockSpec((1, 1, s), lambda b, i, *_: (b, 0, 0), pipeline_mode=once),
            pl.BlockSpec((n_heads, LANES), lambda b, i, *_: (0, 0), pipeline_mode=once),
            pl.BlockSpec((n_heads * tq, 1), lambda b, i, *_: (0, 0), pipeline_mode=once),
        ],
        out_specs=pl.BlockSpec((tq, n_heads * HEAD_DIM), lambda b, i, *_: (b * nq + i, 0)),
        scratch_shapes=[
            pltpu.VMEM((tq, s), I32),
            pltpu.VMEM((N_IDX_HEADS, tq, ts), F32),
            pltpu.VMEM((n_heads * tq, HEAD_DIM), BF16),
            pltpu.VMEM((n_heads * tq, 1), F32),
            pltpu.VMEM((n_heads * tq, 2 * HEAD_DIM), F32),
            pltpu.VMEM((par * hc * tq, tk), F32),
        ],
    )
    return pl.pallas_call(
        kernel,
        grid_spec=grid_spec,
        out_shape=jax.ShapeDtypeStruct((t, n_heads * HEAD_DIM), BF16),
        compiler_params=_params(("parallel", "arbitrary")),
        name="dsa_attention",
    )(far, qkv, qkv, qkv, idx, idx, idx, positions.reshape(t, 1), positions.reshape(batch, 1, s),
      rbt, rbfar)


def _mla_kernel(qn_ref, qr_ref, kn_ref, kr_ref, v_ref, o_ref, kcat_ref, vext_ref, *, tq, tkb, group):
    i = pl.program_id(2)
    wide = 2 * LANES

    @pl.when(i == 0)
    def _():
        ones_col = (lax.broadcasted_iota(I32, (kr_ref.shape[0], LANES), 1) == 0).astype(BF16)
        for h in range(group):
            kcat_ref[:, h * wide:h * wide + LANES] = kn_ref[:, h * LANES:(h + 1) * LANES]
            kcat_ref[:, h * wide + LANES:(h + 1) * wide] = kr_ref[...]
            vext_ref[:, h * wide:h * wide + LANES] = v_ref[:, h * LANES:(h + 1) * LANES]
            vext_ref[:, h * wide + LANES:(h + 1) * wide] = ones_col

    row = lax.broadcasted_iota(I32, (tq, tkb), 0)
    col = lax.broadcasted_iota(I32, (tq, tkb), 1)
    qs = [jnp.concatenate([qn_ref[:, h * LANES:(h + 1) * LANES], qr_ref[:, h * LANES:(h + 1) * LANES]], axis=1)
          for h in range(group)]
    n_full = (i * tq) // tkb

    def chunk(c, state, masked):
        sl = pl.ds(pl.multiple_of(c * tkb, tkb), tkb)
        heads = range(group)
        hs = [slice(h * wide, (h + 1) * wide) for h in heads]
        s = [lax.dot_general(qs[h], kcat_ref[sl, hs[h]], _NT, preferred_element_type=F32) for h in heads]
        if masked:
            s = [jnp.where(c * tkb + col <= i * tq + row, x, NEG_INF) for x in s]
        m_new = [jnp.maximum(state[h][0], jnp.max(s[h], axis=1, keepdims=True)) for h in heads]
        p = [jnp.exp2(s[h] - m_new[h]).astype(BF16) for h in heads]
        pv = [jnp.dot(p[h], vext_ref[sl, hs[h]], preferred_element_type=F32) for h in heads]
        return tuple((m_new[h], jnp.exp2(state[h][0] - m_new[h]) * state[h][1] + pv[h]) for h in heads)

    init = tuple((jnp.full((tq, 1), NEG_INF, F32), jnp.zeros((tq, 2 * LANES), F32)) for _ in range(group))
    state = lax.fori_loop(0, n_full, lambda c, st: chunk(c, st, False), init)
    state = chunk(n_full, state, True)
    for h in range(group):
        acc = state[h][1]
        o_ref[:, h * LANES:(h + 1) * LANES] = (acc[:, 0:V_DIM] / acc[:, V_DIM:V_DIM + 1]).astype(o_ref.dtype)


def mla_attention(qn, qr, kn, kr, v, batch, tq=512, tkb=1024, group=2):
    t = qn.shape[0]
    s = t // batch
    h = qn.shape[1] // QK_NOPE
    tq, tkb, group = min(tq, s), min(tkb, s), min(group, h)
    nq = s // tq
    w = group * LANES
    qspec = pl.BlockSpec((tq, w), lambda b, hh, i: (b * nq + i, hh))
    kspec = pl.BlockSpec((s, w), lambda b, hh, i: (b, hh))
    return pl.pallas_call(
        functools.partial(_mla_kernel, tq=tq, tkb=tkb, group=group),
        grid=(batch, h // group, nq),
        in_specs=[qspec, qspec, kspec, pl.BlockSpec((s, LANES), lambda b, hh, i: (b, 0)), kspec],
        out_specs=qspec,
        out_shape=jax.ShapeDtypeStruct((t, h * V_DIM), BF16),
        scratch_shapes=[pltpu.VMEM((s, 2 * w), BF16),
                        pltpu.VMEM((s, 2 * w), BF16)],
        compiler_params=_params(("parallel", "parallel", "arbitrary")),
        name="mla_attention",
    )(qn, qr, kn, kr, v)


def _first_max(vals):
    best = vals[0]
    for v in vals[1:]:
        best = jnp.maximum(best, v)
    idx = jnp.full(best.shape, len(vals) - 1, I32)
    for k in range(len(vals) - 2, -1, -1):
        idx = jnp.where(vals[k] == best, k, idx)
    return best, idx


def _router_kernel(x_ref, w_ref, b_ref, e_ref, wt_ref):
    logits = lax.dot_general(w_ref[...], x_ref[...], _NT, precision=lax.Precision.HIGHEST,
                             preferred_element_type=F32)
    aff = jax.nn.sigmoid(logits)
    biased = aff + b_ref[...]
    a_rows = [aff[e:e + 1, :] for e in range(N_EXPERTS)]
    b_rows = [biased[e:e + 1, :] for e in range(N_EXPERTS)]
    g_score, g_e1, g_e2 = [], [], []
    for g in range(N_GROUPS):
        vals = b_rows[g * EXPERTS_PER_GROUP:(g + 1) * EXPERTS_PER_GROUP]
        top1, i1 = _first_max(vals)
        rest = [jnp.where(i1 == k, -jnp.inf, vals[k]) for k in range(EXPERTS_PER_GROUP)]
        top2, i2 = _first_max(rest)
        g_score.append(top1 + top2)
        g_e1.append(i1 + g * EXPERTS_PER_GROUP)
        g_e2.append(i2 + g * EXPERTS_PER_GROUP)
    _, grp = _first_max(g_score)
    e1, e2 = g_e1[N_GROUPS - 1], g_e2[N_GROUPS - 1]
    for g in range(N_GROUPS - 2, -1, -1):
        e1 = jnp.where(grp == g, g_e1[g], e1)
        e2 = jnp.where(grp == g, g_e2[g], e2)
    w1 = jnp.zeros_like(a_rows[0])
    w2 = jnp.zeros_like(a_rows[0])
    for e in range(N_EXPERTS):
        w1 = jnp.where(e1 == e, a_rows[e], w1)
        w2 = jnp.where(e2 == e, a_rows[e], w2)
    tot = w1 + w2
    e_ref[0:1, :] = e1
    e_ref[1:2, :] = e2
    wt_ref[0:1, :] = w1 / tot
    wt_ref[1:2, :] = w2 / tot


def router(x, w_router, router_bias, tm=512):
    t, d = x.shape
    tm = min(tm, t)
    return pl.pallas_call(
        _router_kernel,
        grid=(t // tm,),
        in_specs=[pl.BlockSpec((tm, d), lambda i: (i, 0)),
                  pl.BlockSpec((N_EXPERTS, d), lambda i: (0, 0)),
                  pl.BlockSpec((N_EXPERTS, 1), lambda i: (0, 0))],
        out_specs=[pl.BlockSpec((TOP_K_EXPERTS, tm), lambda i: (0, i)),
                   pl.BlockSpec((TOP_K_EXPERTS, tm), lambda i: (0, i))],
        out_shape=[jax.ShapeDtypeStruct((TOP_K_EXPERTS, t), I32),
                   jax.ShapeDtypeStruct((TOP_K_EXPERTS, t), F32)],
        compiler_params=_params(("parallel",)),
        name="router",
    )(x, w_router.T, router_bias.reshape(N_EXPERTS, 1))


def dispatch_plan(experts, tm):
    k, t = experts.shape
    n_slots = k * t
    n_rows = n_slots + N_EXPERTS * tm
    n_tiles = n_rows // tm
    flat = experts.reshape(n_slots)
    order = jnp.argsort(flat, stable=True).astype(I32)
    counts = jnp.zeros((N_EXPERTS,), I32).at[flat].add(1)
    padded = (counts + tm - 1) // tm * tm
    pad_end = jnp.cumsum(padded)
    start = jnp.cumsum(counts) - counts
    e_sorted = flat[order]
    dest_sorted = (pad_end - padded)[e_sorted] + jnp.arange(n_slots, dtype=I32) - start[e_sorted]
    row_token = jnp.zeros((n_rows,), I32).at[dest_sorted].set(order % t)
    dest = jnp.zeros((n_slots,), I32).at[order].set(dest_sorted)
    tile_start = jnp.arange(n_tiles, dtype=I32) * tm
    tile_expert = jnp.minimum(jnp.searchsorted(pad_end, tile_start, side="right"), N_EXPERTS - 1).astype(I32)
    tile_active = (tile_start < pad_end[-1]).astype(I32)
    return row_token.reshape(n_tiles, 1, tm), dest.reshape(k, t), tile_expert, tile_active


def _gather_kernel(act_ref, idx_ref, x_hbm, o_ref, buf, sem, *, tm):
    t = pl.program_id(0)

    def row_copy(r):
        return pltpu.make_async_copy(x_hbm.at[pl.ds(idx_ref[0, 0, r], 1), :], buf.at[pl.ds(r, 1), :], sem)

    @pl.when(act_ref[t] == 1)
    def _():
        def start(r, c):
            row_copy(r).start()
            return c

        def wait(r, c):
            row_copy(r).wait()
            return c

        lax.fori_loop(0, tm, start, 0, unroll=8)
        lax.fori_loop(0, tm, wait, 0, unroll=8)
        o_ref[...] = buf[...].astype(o_ref.dtype)

    @pl.when(act_ref[t] == 0)
    def _():
        o_ref[...] = jnp.zeros_like(o_ref)


def gather_rows(x, row_token, tile_active):
    n_tiles, _, tm = row_token.shape
    d = x.shape[1]
    grid_spec = pltpu.PrefetchScalarGridSpec(
        num_scalar_prefetch=1,
        grid=(n_tiles,),
        in_specs=[pl.BlockSpec((1, 1, tm), lambda t, *_: (t, 0, 0), memory_space=pltpu.SMEM),
                  pl.BlockSpec(memory_space=pl.ANY)],
        out_specs=pl.BlockSpec((tm, d), lambda t, *_: (t, 0)),
        scratch_shapes=[pltpu.VMEM((tm, d), F32), pltpu.SemaphoreType.DMA(())],
    )
    return pl.pallas_call(
        functools.partial(_gather_kernel, tm=tm),
        grid_spec=grid_spec,
        out_shape=jax.ShapeDtypeStruct((n_tiles * tm, d), BF16),
        compiler_params=_params(("arbitrary",)),
        name="gather_rows",
    )(tile_active, row_token, x)


def _ffn_kernel(exp_ref, act_ref, x_ref, wg_ref, wu_ref, wd_ref, o_ref):
    t = pl.program_id(0)

    @pl.when(act_ref[t] == 1)
    def _():
        x = x_ref[...]
        h = jax.nn.silu(jnp.dot(x, wg_ref[0], preferred_element_type=F32)) * jnp.dot(
            x, wu_ref[0], preferred_element_type=F32)
        o_ref[...] = jnp.dot(h.astype(BF16), wd_ref[0], preferred_element_type=F32)

    @pl.when(act_ref[t] == 0)
    def _():
        o_ref[...] = jnp.zeros_like(o_ref)


def expert_ffn(xs, tile_expert, tile_active, w_gate, w_up, w_down, tm):
    n, d = xs.shape
    f = w_gate.shape[2]
    grid_spec = pltpu.PrefetchScalarGridSpec(
        num_scalar_prefetch=2,
        grid=(n // tm,),
        in_specs=[pl.BlockSpec((tm, d), lambda t, ex, ac: (t, 0)),
                  pl.BlockSpec((1, d, f), lambda t, ex, ac: (ex[t], 0, 0)),
                  pl.BlockSpec((1, d, f), lambda t, ex, ac: (ex[t], 0, 0)),
                  pl.BlockSpec((1, f, d), lambda t, ex, ac: (ex[t], 0, 0))],
        out_specs=pl.BlockSpec((tm, d), lambda t, ex, ac: (t, 0)),
    )
    return pl.pallas_call(
        _ffn_kernel,
        grid_spec=grid_spec,
        out_shape=jax.ShapeDtypeStruct((n, d), F32),
        compiler_params=_params(("arbitrary",)),
        name="expert_ffn",
    )(tile_expert, tile_active, xs, w_gate, w_up, w_down)


def _combine_ln_kernel(dest_ref, x_ref, w_ref, g_ref, b_ref, y_hbm, o_ref, ob_ref, buf, sem, *, tm):
    def row_copy(k, r):
        return pltpu.make_async_copy(y_hbm.at[pl.ds(dest_ref[0, k, r], 1), :], buf.at[k, pl.ds(r, 1), :], sem)

    def start(r, c):
        for k in range(TOP_K_EXPERTS):
            row_copy(k, r).start()
        return c

    def wait(r, c):
        for k in range(TOP_K_EXPERTS):
            row_copy(k, r).wait()
        return c

    lax.fori_loop(0, tm, start, 0, unroll=8)
    lax.fori_loop(0, tm, wait, 0, unroll=8)
    h = DEEPNORM_ALPHA * x_ref[...] + (w_ref[:, 0:1] * buf[0] + w_ref[:, 1:2] * buf[1])
    mu = jnp.mean(h, axis=-1, keepdims=True)
    dlt = h - mu
    var = jnp.mean(dlt * dlt, axis=-1, keepdims=True)
    y = dlt * lax.rsqrt(var + LN_EPS) * g_ref[...] + b_ref[...]
    o_ref[...] = y
    ob_ref[...] = y.astype(BF16)


def combine_ln(x, ys, dest, weights, g, b, tm=256):
    t, d = x.shape
    tm = min(tm, t)
    k = dest.shape[0]
    dest_tiles = dest.reshape(k, t // tm, tm).transpose(1, 0, 2)
    row = pl.BlockSpec((tm, d), lambda i: (i, 0))
    vec = pl.BlockSpec((1, d), lambda i: (0, 0))
    return pl.pallas_call(
        functools.partial(_combine_ln_kernel, tm=tm),
        grid=(t // tm,),
        in_specs=[pl.BlockSpec((1, k, tm), lambda i: (i, 0, 0), memory_space=pltpu.SMEM),
                  row,
                  pl.BlockSpec((tm, k), lambda i: (i, 0)),
                  vec, vec,
                  pl.BlockSpec(memory_space=pl.ANY)],
        out_specs=[row, row],
        out_shape=[jax.ShapeDtypeStruct((t, d), F32), jax.ShapeDtypeStruct((t, d), BF16)],
        scratch_shapes=[pltpu.VMEM((k, tm, d), F32), pltpu.SemaphoreType.DMA(())],
        compiler_params=_params(("arbitrary",)),
        name="combine_ln",
    )(dest_tiles, x, weights.T, g.reshape(1, d), b.reshape(1, d), ys)


def even_mixer(xb, positions, w_in, w_out, rel_bias, batch):
    q_scale = HEAD_DIM ** -0.5 * LOG2E
    o0, o1, o2, o3, o4, o5, o6, o7 = (SB_W, 2 * SB_W, 3 * SB_W, 3 * SB_W + DSA_W, 3 * SB_W + DSA_W + HEAD_DIM,
                                      3 * SB_W + DSA_W + 2 * HEAD_DIM,
                                      3 * SB_W + DSA_W + 2 * HEAD_DIM + N_IDX_HEADS * IDX_DIM,
                                      3 * SB_W + DSA_W + 2 * HEAD_DIM + N_IDX_HEADS * IDX_DIM + N_IDX_HEADS)
    w_sb = jnp.concatenate([w_in[:, :o0] * q_scale, w_in[:, o0:o2]], axis=1).astype(BF16)
    w_ds = jnp.concatenate([w_in[:, o2:o3] * q_scale, w_in[:, o3:o5]], axis=1).astype(BF16)
    pad = jnp.zeros((w_in.shape[0], LANES - IDX_DIM - N_IDX_HEADS), w_in.dtype)
    w_ix = jnp.concatenate([w_in[:, o5:o6] * IDX_DIM ** -0.5, w_in[:, o7:], w_in[:, o6:o7], pad],
                           axis=1).astype(BF16)
    qkv_sb = matmul(xb, w_sb, BF16, 1024, 512)
    qkv_ds = matmul(xb, w_ds, BF16, 1024, 768)
    idx = matmul(xb, w_ix, F32, 1024, w_ix.shape[1])
    o_sb = sb_attention(qkv_sb, batch)
    o_ds = dsa_attention(qkv_ds, idx, positions, rel_bias, batch)
    w_out = w_out.astype(BF16)
    return matmul2(o_sb, o_ds, w_out[:SB_W], w_out[SB_W:], F32, 1024, 512)


def mla_mixer(xb, positions, w_down, g_q, g_kv, w_uq, w_ukv, w_o, batch):
    t = xb.shape[0]
    scale = (QK_NOPE + QK_ROPE) ** -0.5 * LOG2E
    w_down = w_down.astype(BF16)
    kr_pad = jnp.zeros((w_down.shape[0], LANES - QK_ROPE), BF16)
    c_q = matmul(xb, w_down[:, :Q_LORA], F32, 1024, 512)
    c_kv = matmul(xb, w_down[:, Q_LORA:Q_LORA + KV_LORA], F32, 1024, 512)
    k_rope = matmul(xb, jnp.concatenate([w_down[:, Q_LORA + KV_LORA:], kr_pad], axis=1), F32, 1024, LANES)
    c_q = rms_norm(c_q, g_q)
    c_kv = rms_norm(c_kv, g_kv)
    w_uq = (w_uq * scale).reshape(Q_LORA, N_HEADS_MLA, QK_NOPE + QK_ROPE)
    w_qn = w_uq[:, :, :QK_NOPE].reshape(Q_LORA, N_HEADS_MLA * QK_NOPE).astype(BF16)
    w_qr = jnp.pad(w_uq[:, :, QK_NOPE:], ((0, 0), (0, 0), (0, LANES - QK_ROPE)))
    w_qr = w_qr.reshape(Q_LORA, N_HEADS_MLA * LANES).astype(BF16)
    w_ukv = w_ukv.reshape(KV_LORA, N_HEADS_MLA, QK_NOPE + V_DIM)
    w_kn = w_ukv[:, :, :QK_NOPE].reshape(KV_LORA, N_HEADS_MLA * QK_NOPE).astype(BF16)
    w_v = w_ukv[:, :, QK_NOPE:].reshape(KV_LORA, N_HEADS_MLA * V_DIM).astype(BF16)
    qn = matmul(c_q, w_qn, BF16, 1024, 1024)
    qr = matmul(c_q, w_qr, F32, 1024, 1024)
    kn = matmul(c_kv, w_kn, BF16, 1024, 1024)
    v = matmul(c_kv, w_v, BF16, 1024, 1024)
    half = QK_ROPE // 2
    inv_freq = ROPE_THETA ** (-jnp.arange(half, dtype=F32) / half)
    inv_freq = jnp.tile(inv_freq, LANES // half).reshape(1, LANES)
    pos_col = positions.reshape(t, 1)
    qr = rope(qr, pos_col, inv_freq)
    kr = rope(k_rope, pos_col, inv_freq)
    o = mla_attention(qn, qr, kn, kr, v, batch)
    return matmul(o, w_o.astype(BF16), F32, 1024, 512)


def moe_ln(x, w_router, router_bias, w_gate, w_up, w_down, g, b, tm=256):
    experts, weights = router(x, w_router, router_bias)
    row_token, dest, tile_expert, tile_active = dispatch_plan(experts, tm)
    xs = gather_rows(x, row_token, tile_active)
    ys = expert_ffn(xs, tile_expert, tile_active, w_gate.astype(BF16), w_up.astype(BF16),
                    w_down.astype(BF16), tm)
    return combine_ln(x, ys, dest, weights, g, b)


def kernel(x, positions, rel_bias, even_w_in, even_w_out, mla_w_down, mla_g_q, mla_g_kv, mla_w_uq,
           mla_w_ukv, mla_w_o, w_router, router_bias, exp_w_gate, exp_w_up, exp_w_down, ln_g, ln_b):
    batch, seq, d = x.shape
    x = x.reshape(batch * seq, d)
    xb = x.astype(BF16)
    for layer in range(DEPTH):
        i = layer // 2
        if layer % 2 == 0:
            mix = even_mixer(xb, positions, even_w_in[i], even_w_out[i], rel_bias, batch)
        else:
            mix = mla_mixer(xb, positions, mla_w_down[i], mla_g_q[i], mla_g_kv[i], mla_w_uq[i],
                            mla_w_ukv[i], mla_w_o[i], batch)
        x, xb = deepnorm_ln(x, mix, ln_g[layer, 0], ln_b[layer, 0])
        x, xb = moe_ln(x, w_router, router_bias, exp_w_gate[layer], exp_w_up[layer], exp_w_down[layer],
                       ln_g[layer, 1], ln_b[layer, 1])
    return x.reshape(batch, seq, d)
```

```python
import functools
import math

import jax
import jax.numpy as jnp
from jax import lax
from jax.experimental import pallas as pl
from jax.experimental.pallas import tpu as pltpu

F32 = jnp.float32
BF16 = jnp.bfloat16
I32 = jnp.int32

D_MODEL = 4096
DEPTH = 4
HEAD_DIM = 128
N_HEADS_SB = 16
N_HEADS_DSA = 16
N_IDX_HEADS = 8
IDX_DIM = 64
TOPK_MAX = 256
N_BUCKETS = 32
MAX_DISTANCE = 128
N_HEADS_MLA = 32
Q_LORA = 1024
KV_LORA = 512
QK_NOPE = 128
QK_ROPE = 64
V_DIM = 128
ROPE_THETA = 10000.0
N_EXPERTS = 16
N_GROUPS = 4
EXPERTS_PER_GROUP = N_EXPERTS // N_GROUPS
TOP_K_EXPERTS = 2
D_FF_EXPERT = 768
DEEPNORM_ALPHA = (2 * DEPTH) ** 0.25
LN_EPS = 1e-5
RMS_EPS = 1e-6
NEG_INF = -1e30
LOG2E = math.log2(math.e)
SB_W = N_HEADS_SB * HEAD_DIM
DSA_W = N_HEADS_DSA * HEAD_DIM

LANES = 128
INT_MIN = -(2 ** 31)
VMEM_LIMIT = 56 * 1024 * 1024

_NT = (((1,), (1,)), ((), ()))


def _params(sem, vmem=VMEM_LIMIT):
    return pltpu.CompilerParams(dimension_semantics=sem, vmem_limit_bytes=vmem)


def _mm_kernel(a_ref, b_ref, o_ref):
    o_ref[...] = jnp.dot(a_ref[...], b_ref[...], preferred_element_type=F32).astype(o_ref.dtype)


def _mm2_kernel(a1_ref, a2_ref, b1_ref, b2_ref, o_ref):
    acc = jnp.dot(a1_ref[...], b1_ref[...], preferred_element_type=F32)
    acc = acc + jnp.dot(a2_ref[...], b2_ref[...], preferred_element_type=F32)
    o_ref[...] = acc.astype(o_ref.dtype)


def matmul(a, b, out_dtype, tm, tn):
    m, k = a.shape
    n = b.shape[1]
    tm, tn = min(tm, m), min(tn, n)
    assert m % tm == 0 and n % tn == 0
    return pl.pallas_call(
        _mm_kernel,
        grid=(m // tm, n // tn),
        in_specs=[pl.BlockSpec((tm, k), lambda i, j: (i, 0)),
                  pl.BlockSpec((k, tn), lambda i, j: (0, j))],
        out_specs=pl.BlockSpec((tm, tn), lambda i, j: (i, j)),
        out_shape=jax.ShapeDtypeStruct((m, n), out_dtype),
        compiler_params=_params(("parallel", "arbitrary")),
        name="matmul",
    )(a, b)


def matmul2(a1, a2, b1, b2, out_dtype, tm, tn):
    m, k1 = a1.shape
    k2 = a2.shape[1]
    n = b1.shape[1]
    tm, tn = min(tm, m), min(tn, n)
    assert m % tm == 0 and n % tn == 0
    return pl.pallas_call(
        _mm2_kernel,
        grid=(m // tm, n // tn),
        in_specs=[pl.BlockSpec((tm, k1), lambda i, j: (i, 0)),
                  pl.BlockSpec((tm, k2), lambda i, j: (i, 0)),
                  pl.BlockSpec((k1, tn), lambda i, j: (0, j)),
                  pl.BlockSpec((k2, tn), lambda i, j: (0, j))],
        out_specs=pl.BlockSpec((tm, tn), lambda i, j: (i, j)),
        out_shape=jax.ShapeDtypeStruct((m, n), out_dtype),
        compiler_params=_params(("parallel", "arbitrary")),
        name="matmul2",
    )(a1, a2, b1, b2)


def _ln_kernel(x_ref, m_ref, g_ref, b_ref, o_ref, ob_ref):
    h = DEEPNORM_ALPHA * x_ref[...] + m_ref[...]
    mu = jnp.mean(h, axis=-1, keepdims=True)
    d = h - mu
    var = jnp.mean(d * d, axis=-1, keepdims=True)
    y = d * lax.rsqrt(var + LN_EPS) * g_ref[...] + b_ref[...]
    o_ref[...] = y
    ob_ref[...] = y.astype(BF16)


def deepnorm_ln(x, mix, g, b, tm=128):
    t, d = x.shape
    tm = min(tm, t)
    row = pl.BlockSpec((tm, d), lambda i: (i, 0))
    vec = pl.BlockSpec((1, d), lambda i: (0, 0))
    return pl.pallas_call(
        _ln_kernel,
        grid=(t // tm,),
        in_specs=[row, row, vec, vec],
        out_specs=[row, row],
        out_shape=[jax.ShapeDtypeStruct((t, d), F32), jax.ShapeDtypeStruct((t, d), BF16)],
        compiler_params=_params(("parallel",)),
        name="deepnorm_ln",
    )(x, mix, g.reshape(1, d), b.reshape(1, d))


def _rms_kernel(x_ref, g_ref, o_ref):
    x = x_ref[...]
    y = x * lax.rsqrt(jnp.mean(x * x, axis=-1, keepdims=True) + RMS_EPS) * g_ref[...]
    o_ref[...] = y.astype(o_ref.dtype)


def rms_norm(x, g, tm=256):
    t, d = x.shape
    tm = min(tm, t)
    return pl.pallas_call(
        _rms_kernel,
        grid=(t // tm,),
        in_specs=[pl.BlockSpec((tm, d), lambda i: (i, 0)), pl.BlockSpec((1, d), lambda i: (0, 0))],
        out_specs=pl.BlockSpec((tm, d), lambda i: (i, 0)),
        out_shape=jax.ShapeDtypeStruct((t, d), BF16),
        compiler_params=_params(("parallel",)),
        name="rms_norm",
    )(x, g.reshape(1, d))


def _rope_kernel(x_ref, pos_ref, f_ref, o_ref):
    lane = lax.broadcasted_iota(I32, (1, LANES), 1)
    first = (lane % QK_ROPE) < (QK_ROPE // 2)
    ang = pos_ref[...].astype(F32) * f_ref[...]
    cos = jnp.cos(ang)
    sin = jnp.sin(ang)
    sin = jnp.where(first, -sin, sin)
    for c in range(x_ref.shape[1] // LANES):
        x = x_ref[:, c * LANES:(c + 1) * LANES]
        partner = jnp.where(first, pltpu.roll(x, LANES - QK_ROPE // 2, 1), pltpu.roll(x, QK_ROPE // 2, 1))
        o_ref[:, c * LANES:(c + 1) * LANES] = (x * cos + partner * sin).astype(o_ref.dtype)


def rope(x, pos_col, inv_freq_lanes, tm=256):
    t, w = x.shape
    tm = min(tm, t)
    return pl.pallas_call(
        _rope_kernel,
        grid=(t // tm,),
        in_specs=[pl.BlockSpec((tm, w), lambda i: (i, 0)),
                  pl.BlockSpec((tm, 1), lambda i: (i, 0)),
                  pl.BlockSpec((1, LANES), lambda i: (0, 0))],
        out_specs=pl.BlockSpec((tm, w), lambda i: (i, 0)),
        out_shape=jax.ShapeDtypeStruct((t, w), BF16),
        compiler_params=_params(("parallel",)),
        name="rope",
    )(x, pos_col, inv_freq_lanes)


def _sb_kernel(q_ref, k_ref, v_ref, o_ref, *, tile, group):
    i = pl.program_id(2)
    row = lax.broadcasted_iota(I32, (tile, tile), 0)
    col = lax.broadcasted_iota(I32, (tile, tile), 1)
    later = (row > col).astype(BF16)
    strict = col < row
    qs = [q_ref[:, h * HEAD_DIM:(h + 1) * HEAD_DIM] for h in range(group)]

    def step(j, state, diag):
        sl = pl.ds(pl.multiple_of(j * tile, tile), tile)
        heads = range(group)
        hs = [slice(h * HEAD_DIM, (h + 1) * HEAD_DIM) for h in heads]
        z = [lax.dot_general(qs[h], k_ref[sl, hs[h]], _NT, preferred_element_type=F32) for h in heads]
        u = [jnp.log2(1.0 + jnp.exp2(-jnp.abs(z[h]))) for h in heads]
        drop = [jnp.maximum(z[h], 0.0) + u[h] for h in heads]
        if diag:
            drop = [jnp.where(strict, d, 0.0) for d in drop]
        after = [jnp.dot(drop[h].astype(BF16), later, preferred_element_type=F32) for h in heads]
        w = [jnp.exp2(jnp.minimum(z[h], 0.0) - u[h] - after[h] - state[h][0]) for h in heads]
        if diag:
            w = [jnp.where(strict, x, 0.0) for x in w]
        pv = [jnp.dot(w[h].astype(BF16), v_ref[sl, hs[h]], preferred_element_type=F32) for h in heads]
        return tuple((state[h][0] + (after[h][:, 0:1] + drop[h][:, 0:1]), state[h][1] + pv[h]) for h in heads)

    init = tuple((jnp.zeros((tile, 1), F32), jnp.zeros((tile, HEAD_DIM), F32)) for _ in range(group))
    state = step(i, init, True)
    state = lax.fori_loop(0, i, lambda n, st: step(i - 1 - n, st, False), state)
    for h in range(group):
        o_ref[:, h * HEAD_DIM:(h + 1) * HEAD_DIM] = state[h][1].astype(o_ref.dtype)


def sb_attention(qkv, batch, tile=256, group=4):
    t = qkv.shape[0]
    s = t // batch
    h = qkv.shape[1] // (3 * HEAD_DIM)
    tile, group = min(tile, s), min(group, h)
    nq = s // tile
    ng = h // group
    w = group * HEAD_DIM
    return pl.pallas_call(
        functools.partial(_sb_kernel, tile=tile, group=group),
        grid=(batch, ng, nq),
        in_specs=[pl.BlockSpec((tile, w), lambda b, hh, i: (b * nq + i, hh)),
                  pl.BlockSpec((s, w), lambda b, hh, i: (b, ng + hh)),
                  pl.BlockSpec((s, w), lambda b, hh, i: (b, 2 * ng + hh))],
        out_specs=pl.BlockSpec((tile, w), lambda b, hh, i: (b * nq + i, hh)),
        out_shape=jax.ShapeDtypeStruct((t, h * HEAD_DIM), BF16),
        compiler_params=_params(("parallel", "parallel", "arbitrary")),
        name="sb_attention",
    )(qkv, qkv, qkv)


def _t5_bucket(dist):
    n = jnp.maximum(dist, 0)
    max_exact = N_BUCKETS // 2
    scaled = jnp.log(jnp.maximum(n, 1).astype(F32) / max_exact) / math.log(MAX_DISTANCE / max_exact)
    large = jnp.minimum(max_exact + (scaled * (N_BUCKETS - max_exact)).astype(I32), N_BUCKETS - 1)
    return jnp.where(n < max_exact, n, large)


def _dsa_kernel(far_ref,
                q_ref, k_ref, v_ref, qi_ref, wk_q_ref, wk_k_ref, posq_ref, posk_ref, rbt_ref, rbfar_ref,
                o_ref,
                keys_ref, wb_ref, qall_ref, m_ref, acc_ref, s_ref,
                *, tq, tk, ts, chunk, rb, nk, topk, n_heads, hc, par):
    b = pl.program_id(0)
    i = pl.program_id(1)
    nq = pl.num_programs(1)
    n_causal = (i + 1) * tq
    n_chunks = (n_causal + chunk - 1) // chunk

    qi = qi_ref[...].astype(BF16)
    for hh in range(N_IDX_HEADS):
        wb_ref[hh] = jnp.broadcast_to(wk_q_ref[:, IDX_DIM + hh:IDX_DIM + hh + 1], (tq, ts))

    def to_key(sc):
        sc = jnp.where(sc == 0.0, 0.0, sc)
        bits = pltpu.bitcast(sc, I32)
        return bits ^ ((bits >> 31) & 0x7FFFFFFF)

    neg_key = to_key(jnp.full((tq, ts), NEG_INF, F32))
    t_idx_s = i * tq + lax.broadcasted_iota(I32, (tq, ts), 0)
    col_s = lax.broadcasted_iota(I32, (tq, ts), 1)

    def score_tile(j, _):
        sl = pl.ds(pl.multiple_of(j * ts, ts), ts)

        @pl.when(j * ts < n_causal)
        def _():
            kj = wk_k_ref[sl, 0:IDX_DIM].astype(BF16)
            sc = jnp.zeros((tq, ts), F32)
            for hh in range(N_IDX_HEADS):
                lg = lax.dot_general(qi[:, hh * IDX_DIM:(hh + 1) * IDX_DIM], kj, _NT,
                                     preferred_element_type=F32)
                sc = sc + wb_ref[hh] * jnp.maximum(lg, 0.0)
            sc = sc * (N_IDX_HEADS ** -0.5)
            sc = jnp.where(j * ts + col_s <= t_idx_s, sc, NEG_INF)
            keys_ref[:, sl] = to_key(sc)

        @pl.when(j * ts >= n_causal)
        def _():
            keys_ref[:, sl] = neg_key

        return 0

    lax.fori_loop(0, n_chunks * (chunk // ts), score_tile, 0)

    n_rb = tq // rb

    def count(pred, ref_val):
        ref_b = [jnp.broadcast_to(ref_val[r * rb:(r + 1) * rb, :], (rb, LANES)) for r in range(n_rb)]

        def body(c, accs):
            accs = list(accs)
            for u in range(chunk // LANES):
                sl = pl.ds(pl.multiple_of(c * chunk + u * LANES, LANES), LANES)
                for r in range(n_rb):
                    hit = pred(keys_ref[r * rb:(r + 1) * rb, sl], ref_b[r])
                    accs[r] = accs[r] + jnp.where(hit, 1, 0)
            return tuple(accs)

        accs = lax.fori_loop(0, n_chunks, body, tuple(jnp.zeros((rb, LANES), I32) for _ in range(n_rb)))
        return jnp.concatenate([jnp.sum(a, axis=1, keepdims=True) for a in accs], axis=0)

    def bisect(it, off):
        trial = off | jnp.left_shift(jnp.int32(1), 31 - it)
        cand = trial + jnp.int32(INT_MIN)
        return jnp.where(count(lambda kk, c: kk >= c, cand) >= topk, trial, off)

    thr = lax.fori_loop(0, 32, bisect, jnp.zeros((tq, 1), I32)) + jnp.int32(INT_MIN)
    need = (topk - count(lambda kk, c: kk > c, thr)).astype(F32)

    for hh in range(n_heads):
        qall_ref[hh * tq:(hh + 1) * tq, :] = q_ref[:, hh * HEAD_DIM:(hh + 1) * HEAD_DIM]
    m_ref[...] = jnp.full(m_ref.shape, NEG_INF, F32)
    acc_ref[...] = jnp.zeros(acc_ref.shape, F32)
    r2 = lax.broadcasted_iota(I32, (tk, tk), 0)
    c2 = lax.broadcasted_iota(I32, (tk, tk), 1)
    earlier = (r2 < c2).astype(BF16)
    ones_col = (lax.broadcasted_iota(I32, (tk, LANES), 1) == 0).astype(BF16)
    pos_q = posq_ref[...]
    n_hc = n_heads // hc
    rows_c = hc * tq
    n_tiles = (n_causal + tk - 1) // tk
    t_idx = i * tq + lax.broadcasted_iota(I32, (tq, tk), 0)
    col = lax.broadcasted_iota(I32, (tq, tk), 1)

    def attend_tile(j, eq_before):
        sl = pl.ds(pl.multiple_of(j * tk, tk), tk)
        kt = keys_ref[:, sl]
        eq = jnp.where(kt == thr, 1.0, 0.0)
        rank = eq_before + jnp.dot(eq.astype(BF16), earlier, preferred_element_type=F32)
        take = jnp.where(kt > thr, 1.0, jnp.where(rank < need, eq, 0.0))
        mask = jnp.where(j * tk + col <= t_idx, take, 0.0) > 0.5
        is_far = far_ref[(b * nq + i) * nk + j] == 1
        k_t = k_ref[sl, :]
        v_ext = jnp.concatenate([v_ref[sl, :], ones_col], axis=1)

        def near_bias(c):
            bucket = _t5_bucket(pos_q - posk_ref[0, :, sl])
            per_head = []
            for hh in range(c * hc, (c + 1) * hc):
                tbl = jnp.broadcast_to(rbt_ref[hh:hh + 1, :], (tq, LANES))
                halves = [jnp.take_along_axis(tbl, bucket[:, x * LANES:(x + 1) * LANES], axis=1,
                                              mode="promise_in_bounds") for x in range(tk // LANES)]
                per_head.append(jnp.concatenate(halves, axis=1))
            return jnp.stack(per_head, axis=0)

        for g in range(n_hc // par):
            chunks = range(par)
            rows = [slice((g * par + c) * rows_c, (g * par + c + 1) * rows_c) for c in chunks]
            local = [slice(c * rows_c, (c + 1) * rows_c) for c in chunks]
            for c in chunks:
                s_ref[local[c], :] = lax.dot_general(qall_ref[rows[c], :], k_t, _NT, preferred_element_type=F32)

            @pl.when(jnp.logical_not(is_far))
            def _():
                for c in chunks:
                    s_ref[local[c], :] += near_bias(g * par + c).reshape(rows_c, tk)

            s = [jnp.where(mask[None], s_ref[local[c], :].reshape(hc, tq, tk), NEG_INF).reshape(rows_c, tk)
                 for c in chunks]
            shift = [jnp.where(is_far, rbfar_ref[rows[c], :], 0.0) for c in chunks]
            m_old = [m_ref[rows[c], :] for c in chunks]
            m_new = [jnp.maximum(m_old[c], jnp.max(s[c], axis=1, keepdims=True) + shift[c]) for c in chunks]
            p = [jnp.exp2(s[c] - (m_new[c] - shift[c])).astype(BF16) for c in chunks]
            pv = [jnp.dot(p[c], v_ext, preferred_element_type=F32) for c in chunks]
            for c in chunks:
                acc_ref[rows[c], :] = jnp.exp2(m_old[c] - m_new[c]) * acc_ref[rows[c], :] + pv[c]
                m_ref[rows[c], :] = m_new[c]
        return eq_before + jnp.sum(eq, axis=1, keepdims=True)

    lax.fori_loop(0, n_tiles, attend_tile, jnp.zeros((tq, 1), F32))
    for hh in range(n_heads):
        rows = slice(hh * tq, (hh + 1) * tq)
        o_ref[:, hh * HEAD_DIM:(hh + 1) * HEAD_DIM] = (
            acc_ref[rows, 0:HEAD_DIM] / acc_ref[rows, HEAD_DIM:HEAD_DIM + 1]).astype(o_ref.dtype)


def dsa_attention(qkv, idx, positions, rel_bias, batch, tq=256, tk=512, ts=256, chunk=1024, rb=128, hc=4,
                  par=2):
    t = qkv.shape[0]
    s = t // batch
    n_heads = (qkv.shape[1] - 2 * HEAD_DIM) // HEAD_DIM
    tq, tk, ts, chunk, rb = min(tq, s), min(tk, s), min(ts, s), min(chunk, s), min(rb, tq)
    assert chunk % tk == 0 and chunk % ts == 0 and s % chunk == 0 and tq % rb == 0
    hc = min(hc, n_heads)
    par = min(par, n_heads // hc)
    nq, nk = s // tq, s // tk
    topk = min(TOPK_MAX, s // 4)
    qi_w = N_IDX_HEADS * IDX_DIM
    pos_lo = positions.reshape(batch, nq, tq).min(axis=-1)
    pos_hi = positions.reshape(batch, nk, tk).max(axis=-1)
    far = (pos_lo[:, :, None] - pos_hi[:, None, :] >= MAX_DISTANCE).astype(I32).reshape(-1)
    rb2 = rel_bias.astype(F32) * LOG2E
    rbt = jnp.pad(rb2.T, ((0, 0), (0, LANES - N_BUCKETS)))
    rbfar = jnp.repeat(rb2[N_BUCKETS - 1], tq).reshape(n_heads * tq, 1)
    kernel = functools.partial(_dsa_kernel, tq=tq, tk=tk, ts=ts, chunk=chunk, rb=rb, nk=nk, topk=topk,
                               n_heads=n_heads, hc=hc, par=par)
    once = pl.Buffered(1)
    grid_spec = pltpu.PrefetchScalarGridSpec(
        num_scalar_prefetch=1,
        grid=(batch, nq),
        in_specs=[
            pl.BlockSpec((tq, n_heads * HEAD_DIM), lambda b, i, *_: (b * nq + i, 0)),
            pl.BlockSpec((s, HEAD_DIM), lambda b, i, *_: (b, n_heads), pipeline_mode=once),
            pl.BlockSpec((s, HEAD_DIM), lambda b, i, *_: (b, n_heads + 1), pipeline_mode=once),
            pl.BlockSpec((tq, qi_w), lambda b, i, *_: (b * nq + i, 0)),
            pl.BlockSpec((tq, LANES), lambda b, i, *_: (b * nq + i, qi_w // LANES)),
            pl.BlockSpec((s, LANES), lambda b, i, *_: (b, qi_w // LANES), pipeline_mode=once),
            pl.BlockSpec((tq, 1), lambda b, i, *_: (b * nq + i, 0)),
            pl.BlockSpec((1, 1, s), lambda b, i, *_: (b, 0, 0), pipeline_mode=once),
            pl.BlockSpec((n_heads, LANES), lambda b, i, *_: (0, 0), pipeline_mode=once),
            pl.BlockSpec((n_heads * tq, 1), lambda b, i, *_: (0, 0), pipeline_mode=once),
        ],
        out_specs=pl.BlockSpec((tq, n_heads * HEAD_DIM), lambda b, i, *_: (b * nq + i, 0)),
        scratch_shapes=[
            pltpu.VMEM((tq, s), I32),
            pltpu.VMEM((N_IDX_HEADS, tq, ts), F32),
            pltpu.VMEM((n_heads * tq, HEAD_DIM), BF16),
            pltpu.VMEM((n_heads * tq, 1), F32),
            pltpu.VMEM((n_heads * tq, 2 * HEAD_DIM), F32),
            pltpu.VMEM((par * hc * tq, tk), F32),
        ],
    )
    return pl.pallas_call(
        kernel,
        grid_spec=grid_spec,
        out_shape=jax.ShapeDtypeStruct((t, n_heads * HEAD_DIM), BF16),
        compiler_params=_params(("parallel", "arbitrary")),
        name="dsa_attention",
    )(far, qkv, qkv, qkv, idx, idx, idx, positions.reshape(t, 1), positions.reshape(batch, 1, s),
      rbt, rbfar)


def _mla_kernel(qn_ref, qr_ref, kn_ref, kr_ref, v_ref, o_ref, kcat_ref, vext_ref, *, tq, tkb, group):
    i = pl.program_id(2)
    wide = 2 * LANES

    @pl.when(i == 0)
    def _():
        ones_col = (lax.broadcasted_iota(I32, (kr_ref.shape[0], LANES), 1) == 0).astype(BF16)
        for h in range(group):
            kcat_ref[:, h * wide:h * wide + LANES] = kn_ref[:, h * LANES:(h + 1) * LANES]
            kcat_ref[:, h * wide + LANES:(h + 1) * wide] = kr_ref[...]
            vext_ref[:, h * wide:h * wide + LANES] = v_ref[:, h * LANES:(h + 1) * LANES]
            vext_ref[:, h * wide + LANES:(h + 1) * wide] = ones_col

    row = lax.broadcasted_iota(I32, (tq, tkb), 0)
    col = lax.broadcasted_iota(I32, (tq, tkb), 1)
    qs = [jnp.concatenate([qn_ref[:, h * LANES:(h + 1) * LANES], qr_ref[:, h * LANES:(h + 1) * LANES]], axis=1)
          for h in range(group)]
    n_full = (i * tq) // tkb

    def chunk(c, state, masked):
        sl = pl.ds(pl.multiple_of(c * tkb, tkb), tkb)
        heads = range(group)
        hs = [slice(h * wide, (h + 1) * wide) for h in heads]
        s = [lax.dot_general(qs[h], kcat_ref[sl, hs[h]], _NT, preferred_element_type=F32) for h in heads]
        if masked:
            s = [jnp.where(c * tkb + col <= i * tq + row, x, NEG_INF) for x in s]
        m_new = [jnp.maximum(state[h][0], jnp.max(s[h], axis=1, keepdims=True)) for h in heads]
        p = [jnp.exp2(s[h] - m_new[h]).astype(BF16) for h in heads]
        pv = [jnp.dot(p[h], vext_ref[sl, hs[h]], preferred_element_type=F32) for h in heads]
        return tuple((m_new[h], jnp.exp2(state[h][0] - m_new[h]) * state[h][1] + pv[h]) for h in heads)

    init = tuple((jnp.full((tq, 1), NEG_INF, F32), jnp.zeros((tq, 2 * LANES), F32)) for _ in range(group))
    state = lax.fori_loop(0, n_full, lambda c, st: chunk(c, st, False), init)
    state = chunk(n_full, state, True)
    for h in range(group):
        acc = state[h][1]
        o_ref[:, h * LANES:(h + 1) * LANES] = (acc[:, 0:V_DIM] / acc[:, V_DIM:V_DIM + 1]).astype(o_ref.dtype)


def mla_attention(qn, qr, kn, kr, v, batch, tq=512, tkb=1024, group=2):
    t = qn.shape[0]
    s = t // batch
    h = qn.shape[1] // QK_NOPE
    tq, tkb, group = min(tq, s), min(tkb, s), min(group, h)
    nq = s // tq
    w = group * LANES
    qspec = pl.BlockSpec((tq, w), lambda b, hh, i: (b * nq + i, hh))
    kspec = pl.BlockSpec((s, w), lambda b, hh, i: (b, hh))
    return pl.pallas_call(
        functools.partial(_mla_kernel, tq=tq, tkb=tkb, group=group),
        grid=(batch, h // group, nq),
        in_specs=[qspec, qspec, kspec, pl.BlockSpec((s, LANES), lambda b, hh, i: (b, 0)), kspec],
        out_specs=qspec,
        out_shape=jax.ShapeDtypeStruct((t, h * V_DIM), BF16),
        scratch_shapes=[pltpu.VMEM((s, 2 * w), BF16),
                        pltpu.VMEM((s, 2 * w), BF16)],
        compiler_params=_params(("parallel", "parallel", "arbitrary")),
        name="mla_attention",
    )(qn, qr, kn, kr, v)


def _first_max(vals):
    best = vals[0]
    for v in vals[1:]:
        best = jnp.maximum(best, v)
    idx = jnp.full(best.shape, len(vals) - 1, I32)
    for k in range(len(vals) - 2, -1, -1):
        idx = jnp.where(vals[k] == best, k, idx)
    return best, idx


def _router_kernel(x_ref, w_ref, b_ref, e_ref, wt_ref):
    logits = lax.dot_general(w_ref[...], x_ref[...], _NT, precision=lax.Precision.HIGHEST,
                             preferred_element_type=F32)
    aff = jax.nn.sigmoid(logits)
    biased = aff + b_ref[...]
    a_rows = [aff[e:e + 1, :] for e in range(N_EXPERTS)]
    b_rows = [biased[e:e + 1, :] for e in range(N_EXPERTS)]
    g_score, g_e1, g_e2 = [], [], []
    for g in range(N_GROUPS):
        vals = b_rows[g * EXPERTS_PER_GROUP:(g + 1) * EXPERTS_PER_GROUP]
        top1, i1 = _first_max(vals)
        rest = [jnp.where(i1 == k, -jnp.inf, vals[k]) for k in range(EXPERTS_PER_GROUP)]
        top2, i2 = _first_max(rest)
        g_score.append(top1 + top2)
        g_e1.append(i1 + g * EXPERTS_PER_GROUP)
        g_e2.append(i2 + g * EXPERTS_PER_GROUP)
    _, grp = _first_max(g_score)
    e1, e2 = g_e1[N_GROUPS - 1], g_e2[N_GROUPS - 1]
    for g in range(N_GROUPS - 2, -1, -1):
        e1 = jnp.where(grp == g, g_e1[g], e1)
        e2 = jnp.where(grp == g, g_e2[g], e2)
    w1 = jnp.zeros_like(a_rows[0])
    w2 = jnp.zeros_like(a_rows[0])
    for e in range(N_EXPERTS):
        w1 = jnp.where(e1 == e, a_rows[e], w1)
        w2 = jnp.where(e2 == e, a_rows[e], w2)
    tot = w1 + w2
    e_ref[0:1, :] = e1
    e_ref[1:2, :] = e2
    wt_ref[0:1, :] = w1 / tot
    wt_ref[1:2, :] = w2 / tot


def router(x, w_router, router_bias, tm=512):
    t, d = x.shape
    tm = min(tm, t)
    return pl.pallas_call(
        _router_kernel,
        grid=(t // tm,),
        in_specs=[pl.BlockSpec((tm, d), lambda i: (i, 0)),
                  pl.BlockSpec((N_EXPERTS, d), lambda i: (0, 0)),
                  pl.BlockSpec((N_EXPERTS, 1), lambda i: (0, 0))],
        out_specs=[pl.BlockSpec((TOP_K_EXPERTS, tm), lambda i: (0, i)),
                   pl.BlockSpec((TOP_K_EXPERTS, tm), lambda i: (0, i))],
        out_shape=[jax.ShapeDtypeStruct((TOP_K_EXPERTS, t), I32),
                   jax.ShapeDtypeStruct((TOP_K_EXPERTS, t), F32)],
        compiler_params=_params(("parallel",)),
        name="router",
    )(x, w_router.T, router_bias.reshape(N_EXPERTS, 1))


def dispatch_plan(experts, tm):
    k, t = experts.shape
    n_slots = k * t
    n_rows = n_slots + N_EXPERTS * tm
    n_tiles = n_rows // tm
    flat = experts.reshape(n_slots)
    order = jnp.argsort(flat, stable=True).astype(I32)
    counts = jnp.zeros((N_EXPERTS,), I32).at[flat].add(1)
    padded = (counts + tm - 1) // tm * tm
    pad_end = jnp.cumsum(padded)
    start = jnp.cumsum(counts) - counts
    e_sorted = flat[order]
    dest_sorted = (pad_end - padded)[e_sorted] + jnp.arange(n_slots, dtype=I32) - start[e_sorted]
    row_token = jnp.zeros((n_rows,), I32).at[dest_sorted].set(order % t)
    dest = jnp.zeros((n_slots,), I32).at[order].set(dest_sorted)
    tile_start = jnp.arange(n_tiles, dtype=I32) * tm
    tile_expert = jnp.minimum(jnp.searchsorted(pad_end, tile_start, side="right"), N_EXPERTS - 1).astype(I32)
    tile_active = (tile_start < pad_end[-1]).astype(I32)
    return row_token.reshape(n_tiles, 1, tm), dest.reshape(k, t), tile_expert, tile_active


def _ffn_kernel(exp_ref, act_ref, idx_ref, idx_next_ref, x_hbm, wg_ref, wu_ref, wd_ref, o_ref,
                land, xs, sem, *, tm):
    t = pl.program_id(0)
    last = pl.num_programs(0) - 1

    def row_copy(ids, r):
        return pltpu.make_async_copy(x_hbm.at[pl.ds(ids[0, 0, r], 1), :], land.at[pl.ds(r, 1), :], sem)

    def start_tile(ids):
        def body(r, c):
            row_copy(ids, r).start()
            return c
        lax.fori_loop(0, tm, body, 0, unroll=8)

    def wait_tile(ids):
        def body(r, c):
            row_copy(ids, r).wait()
            return c
        lax.fori_loop(0, tm, body, 0, unroll=8)

    @pl.when(jnp.logical_and(t == 0, act_ref[0] == 1))
    def _():
        start_tile(idx_ref)

    @pl.when(act_ref[t] == 1)
    def _():
        wait_tile(idx_ref)
        xs[...] = land[...].astype(BF16)

    @pl.when(jnp.logical_and(t < last, act_ref[jnp.minimum(t + 1, last)] == 1))
    def _():
        start_tile(idx_next_ref)

    @pl.when(act_ref[t] == 1)
    def _():
        x = xs[...]
        h = jax.nn.silu(jnp.dot(x, wg_ref[0], preferred_element_type=F32)) * jnp.dot(
            x, wu_ref[0], preferred_element_type=F32)
        o_ref[...] = jnp.dot(h.astype(BF16), wd_ref[0], preferred_element_type=F32)

    @pl.when(act_ref[t] == 0)
    def _():
        o_ref[...] = jnp.zeros_like(o_ref)


def expert_ffn(x, row_token, tile_expert, tile_active, w_gate, w_up, w_down):
    n_tiles, _, tm = row_token.shape
    d = x.shape[1]
    f = w_gate.shape[2]
    ids = lambda shift: pl.BlockSpec(
        (1, 1, tm), lambda t, ex, ac: (jnp.minimum(t + shift, n_tiles - 1), 0, 0), memory_space=pltpu.SMEM)
    grid_spec = pltpu.PrefetchScalarGridSpec(
        num_scalar_prefetch=2,
        grid=(n_tiles,),
        in_specs=[ids(0), ids(1),
                  pl.BlockSpec(memory_space=pl.ANY),
                  pl.BlockSpec((1, d, f), lambda t, ex, ac: (ex[t], 0, 0)),
                  pl.BlockSpec((1, d, f), lambda t, ex, ac: (ex[t], 0, 0)),
                  pl.BlockSpec((1, f, d), lambda t, ex, ac: (ex[t], 0, 0))],
        out_specs=pl.BlockSpec((tm, d), lambda t, ex, ac: (t, 0)),
        scratch_shapes=[pltpu.VMEM((tm, d), F32),
                        pltpu.VMEM((tm, d), BF16),
                        pltpu.SemaphoreType.DMA(())],
    )
    return pl.pallas_call(
        functools.partial(_ffn_kernel, tm=tm),
        grid_spec=grid_spec,
        out_shape=jax.ShapeDtypeStruct((n_tiles * tm, d), F32),
        compiler_params=_params(("arbitrary",)),
        name="expert_ffn",
    )(tile_expert, tile_active, row_token, row_token, x, w_gate, w_up, w_down)


def _combine_ln_kernel(dest_ref, dest_next_ref, x_ref, w_ref, g_ref, b_ref, y_hbm, o_ref, ob_ref, buf, sem,
                       *, tm):
    i = pl.program_id(0)
    last = pl.num_programs(0) - 1
    slot = i % 2

    def row_copy(ids, k, r, s):
        return pltpu.make_async_copy(y_hbm.at[pl.ds(ids[0, k, r], 1), :], buf.at[s, k, pl.ds(r, 1), :],
                                     sem.at[s])

    def start_tile(ids, s):
        def body(r, c):
            for k in range(TOP_K_EXPERTS):
                row_copy(ids, k, r, s).start()
            return c
        lax.fori_loop(0, tm, body, 0, unroll=8)

    @pl.when(i == 0)
    def _():
        start_tile(dest_ref, 0)

    @pl.when(i < last)
    def _():
        start_tile(dest_next_ref, 1 - slot)

    def wait(r, c):
        for k in range(TOP_K_EXPERTS):
            row_copy(dest_ref, k, r, slot).wait()
        return c

    lax.fori_loop(0, tm, wait, 0, unroll=8)
    h = DEEPNORM_ALPHA * x_ref[...] + (w_ref[:, 0:1] * buf[slot, 0] + w_ref[:, 1:2] * buf[slot, 1])
    mu = jnp.mean(h, axis=-1, keepdims=True)
    dlt = h - mu
    var = jnp.mean(dlt * dlt, axis=-1, keepdims=True)
    y = dlt * lax.rsqrt(var + LN_EPS) * g_ref[...] + b_ref[...]
    o_ref[...] = y
    ob_ref[...] = y.astype(BF16)


def combine_ln(x, ys, dest, weights, g, b, tm=256):
    t, d = x.shape
    tm = min(tm, t)
    k = dest.shape[0]
    n = t // tm
    dest_tiles = dest.reshape(k, n, tm).transpose(1, 0, 2)
    row = pl.BlockSpec((tm, d), lambda i: (i, 0))
    vec = pl.BlockSpec((1, d), lambda i: (0, 0))
    ids = lambda shift: pl.BlockSpec((1, k, tm), lambda i: (jnp.minimum(i + shift, n - 1), 0, 0),
                                     memory_space=pltpu.SMEM)
    return pl.pallas_call(
        functools.partial(_combine_ln_kernel, tm=tm),
        grid=(n,),
        in_specs=[ids(0), ids(1),
                  row,
                  pl.BlockSpec((tm, k), lambda i: (i, 0)),
                  vec, vec,
                  pl.BlockSpec(memory_space=pl.ANY)],
        out_specs=[row, row],
        out_shape=[jax.ShapeDtypeStruct((t, d), F32), jax.ShapeDtypeStruct((t, d), BF16)],
        scratch_shapes=[pltpu.VMEM((2, k, tm, d), F32), pltpu.SemaphoreType.DMA((2,))],
        compiler_params=_params(("arbitrary",)),
        name="combine_ln",
    )(dest_tiles, dest_tiles, x, weights.T, g.reshape(1, d), b.reshape(1, d), ys)


def even_mixer(xb, positions, w_in, w_out, rel_bias, batch):
    q_scale = HEAD_DIM ** -0.5 * LOG2E
    o0, o1, o2, o3, o4, o5, o6, o7 = (SB_W, 2 * SB_W, 3 * SB_W, 3 * SB_W + DSA_W, 3 * SB_W + DSA_W + HEAD_DIM,
                                      3 * SB_W + DSA_W + 2 * HEAD_DIM,
                                      3 * SB_W + DSA_W + 2 * HEAD_DIM + N_IDX_HEADS * IDX_DIM,
                                      3 * SB_W + DSA_W + 2 * HEAD_DIM + N_IDX_HEADS * IDX_DIM + N_IDX_HEADS)
    w_sb = jnp.concatenate([w_in[:, :o0] * q_scale, w_in[:, o0:o2]], axis=1).astype(BF16)
    w_ds = jnp.concatenate([w_in[:, o2:o3] * q_scale, w_in[:, o3:o5]], axis=1).astype(BF16)
    pad = jnp.zeros((w_in.shape[0], LANES - IDX_DIM - N_IDX_HEADS), w_in.dtype)
    w_ix = jnp.concatenate([w_in[:, o5:o6] * IDX_DIM ** -0.5, w_in[:, o7:], w_in[:, o6:o7], pad],
                           axis=1).astype(BF16)
    qkv_sb = matmul(xb, w_sb, BF16, 1024, 512)
    qkv_ds = matmul(xb, w_ds, BF16, 1024, 768)
    idx = matmul(xb, w_ix, F32, 1024, w_ix.shape[1])
    o_sb = sb_attention(qkv_sb, batch)
    o_ds = dsa_attention(qkv_ds, idx, positions, rel_bias, batch)
    w_out = w_out.astype(BF16)
    return matmul2(o_sb, o_ds, w_out[:SB_W], w_out[SB_W:], F32, 1024, 512)


def mla_mixer(xb, positions, w_down, g_q, g_kv, w_uq, w_ukv, w_o, batch):
    t = xb.shape[0]
    scale = (QK_NOPE + QK_ROPE) ** -0.5 * LOG2E
    w_down = w_down.astype(BF16)
    kr_pad = jnp.zeros((w_down.shape[0], LANES - QK_ROPE), BF16)
    c_q = matmul(xb, w_down[:, :Q_LORA], F32, 1024, 512)
    c_kv = matmul(xb, w_down[:, Q_LORA:Q_LORA + KV_LORA], F32, 1024, 512)
    k_rope = matmul(xb, jnp.concatenate([w_down[:, Q_LORA + KV_LORA:], kr_pad], axis=1), F32, 1024, LANES)
    c_q = rms_norm(c_q, g_q)
    c_kv = rms_norm(c_kv, g_kv)
    w_uq = (w_uq * scale).reshape(Q_LORA, N_HEADS_MLA, QK_NOPE + QK_ROPE)
    w_qn = w_uq[:, :, :QK_NOPE].reshape(Q_LORA, N_HEADS_MLA * QK_NOPE).astype(BF16)
    w_qr = jnp.pad(w_uq[:, :, QK_NOPE:], ((0, 0), (0, 0), (0, LANES - QK_ROPE)))
    w_qr = w_qr.reshape(Q_LORA, N_HEADS_MLA * LANES).astype(BF16)
    w_ukv = w_ukv.reshape(KV_LORA, N_HEADS_MLA, QK_NOPE + V_DIM)
    w_kn = w_ukv[:, :, :QK_NOPE].reshape(KV_LORA, N_HEADS_MLA * QK_NOPE).astype(BF16)
    w_v = w_ukv[:, :, QK_NOPE:].reshape(KV_LORA, N_HEADS_MLA * V_DIM).astype(BF16)
    qn = matmul(c_q, w_qn, BF16, 1024, 1024)
    qr = matmul(c_q, w_qr, F32, 1024, 1024)
    kn = matmul(c_kv, w_kn, BF16, 1024, 1024)
    v = matmul(c_kv, w_v, BF16, 1024, 1024)
    half = QK_ROPE // 2
    inv_freq = ROPE_THETA ** (-jnp.arange(half, dtype=F32) / half)
    inv_freq = jnp.tile(inv_freq, LANES // half).reshape(1, LANES)
    pos_col = positions.reshape(t, 1)
    qr = rope(qr, pos_col, inv_freq)
    kr = rope(k_rope, pos_col, inv_freq)
    o = mla_attention(qn, qr, kn, kr, v, batch)
    return matmul(o, w_o.astype(BF16), F32, 1024, 512)


def moe_ln(x, w_router, router_bias, w_gate, w_up, w_down, g, b, tm=256):
    experts, weights = router(x, w_router, router_bias)
    row_token, dest, tile_expert, tile_active = dispatch_plan(experts, tm)
    ys = expert_ffn(x, row_token, tile_expert, tile_active, w_gate.astype(BF16), w_up.astype(BF16),
                    w_down.astype(BF16))
    return combine_ln(x, ys, dest, weights, g, b)


def kernel(x, positions, rel_bias, even_w_in, even_w_out, mla_w_down, mla_g_q, mla_g_kv, mla_w_uq,
           mla_w_ukv, mla_w_o, w_router, router_bias, exp_w_gate, exp_w_up, exp_w_down, ln_g, ln_b):
    batch, seq, d = x.shape
    x = x.reshape(batch * seq, d)
    xb = x.astype(BF16)
    for layer in range(DEPTH):
        i = layer // 2
        if layer % 2 == 0:
            mix = even_mixer(xb, positions, even_w_in[i], even_w_out[i], rel_bias, batch)
        else:
            mix = mla_mixer(xb, positions, mla_w_down[i], mla_g_q[i], mla_g_kv[i], mla_w_uq[i],
                            mla_w_ukv[i], mla_w_o[i], batch)
        x, xb = deepnorm_ln(x, mix, ln_g[layer, 0], ln_b[layer, 0])
        x, xb = moe_ln(x, w_router, router_bias, exp_w_gate[layer], exp_w_up[layer], exp_w_down[layer],
                       ln_g[layer, 1], ln_b[layer, 1])
    return x.reshape(batch, seq, d)
```

```python
import functools
import math

import jax
import jax.numpy as jnp
from jax import lax
from jax.experimental import pallas as pl
from jax.experimental.pallas import tpu as pltpu

F32 = jnp.float32
BF16 = jnp.bfloat16
I32 = jnp.int32

D_MODEL = 4096
DEPTH = 4
HEAD_DIM = 128
N_HEADS_SB = 16
N_HEADS_DSA = 16
N_IDX_HEADS = 8
IDX_DIM = 64
TOPK_MAX = 256
N_BUCKETS = 32
MAX_DISTANCE = 128
N_HEADS_MLA = 32
Q_LORA = 1024
KV_LORA = 512
QK_NOPE = 128
QK_ROPE = 64
V_DIM = 128
ROPE_THETA = 10000.0
N_EXPERTS = 16
N_GROUPS = 4
EXPERTS_PER_GROUP = N_EXPERTS // N_GROUPS
TOP_K_EXPERTS = 2
D_FF_EXPERT = 768
DEEPNORM_ALPHA = (2 * DEPTH) ** 0.25
LN_EPS = 1e-5
RMS_EPS = 1e-6
NEG_INF = -1e30
LOG2E = math.log2(math.e)
SB_W = N_HEADS_SB * HEAD_DIM
DSA_W = N_HEADS_DSA * HEAD_DIM

LANES = 128
INT_MIN = -(2 ** 31)
VMEM_LIMIT = 56 * 1024 * 1024

_NT = (((1,), (1,)), ((), ()))


def _params(sem, vmem=VMEM_LIMIT):
    return pltpu.CompilerParams(dimension_semantics=sem, vmem_limit_bytes=vmem)


def _mm_kernel(a_ref, b_ref, o_ref):
    o_ref[...] = jnp.dot(a_ref[...], b_ref[...], preferred_element_type=F32).astype(o_ref.dtype)


def _mm2_kernel(a1_ref, a2_ref, b1_ref, b2_ref, o_ref):
    acc = jnp.dot(a1_ref[...], b1_ref[0].astype(BF16), preferred_element_type=F32)
    acc = acc + jnp.dot(a2_ref[...], b2_ref[0].astype(BF16), preferred_element_type=F32)
    o_ref[...] = acc.astype(o_ref.dtype)


def matmul(a, b, out_dtype, tm, tn):
    m, k = a.shape
    n = b.shape[1]
    tm, tn = min(tm, m), min(tn, n)
    assert m % tm == 0 and n % tn == 0
    return pl.pallas_call(
        _mm_kernel,
        grid=(m // tm, n // tn),
        in_specs=[pl.BlockSpec((tm, k), lambda i, j: (i, 0)),
                  pl.BlockSpec((k, tn), lambda i, j: (0, j))],
        out_specs=pl.BlockSpec((tm, tn), lambda i, j: (i, j)),
        out_shape=jax.ShapeDtypeStruct((m, n), out_dtype),
        compiler_params=_params(("parallel", "arbitrary")),
        name="matmul",
    )(a, b)


def _mm_w32_kernel(a_ref, b_ref, s_ref, o_ref):
    b = (b_ref[0] * s_ref[...]).astype(BF16)
    o_ref[...] = jnp.dot(a_ref[...], b, preferred_element_type=F32).astype(o_ref.dtype)


def matmul_w32(a, w, layer, col0, n, col_scale, out_dtype, tm, tn):
    m, k = a.shape
    tm = min(tm, m)
    assert m % tm == 0 and n % tn == 0 and col0 % tn == 0
    j0 = col0 // tn
    return pl.pallas_call(
        _mm_w32_kernel,
        grid=(m // tm, n // tn),
        in_specs=[pl.BlockSpec((tm, k), lambda i, j: (i, 0)),
                  pl.BlockSpec((1, k, tn), lambda i, j: (layer, 0, j0 + j)),
                  pl.BlockSpec((1, tn), lambda i, j: (0, j))],
        out_specs=pl.BlockSpec((tm, tn), lambda i, j: (i, j)),
        out_shape=jax.ShapeDtypeStruct((m, n), out_dtype),
        compiler_params=_params(("parallel", "arbitrary")),
        name="matmul_w32",
    )(a, w, col_scale.reshape(1, n).astype(F32))


def matmul2(a1, a2, w, layer, out_dtype, tm, tn):
    m, k1 = a1.shape
    k2 = a2.shape[1]
    n = w.shape[2]
    tm, tn = min(tm, m), min(tn, n)
    assert m % tm == 0 and n % tn == 0 and k1 == k2 and w.shape[1] == k1 + k2
    return pl.pallas_call(
        _mm2_kernel,
        grid=(m // tm, n // tn),
        in_specs=[pl.BlockSpec((tm, k1), lambda i, j: (i, 0)),
                  pl.BlockSpec((tm, k2), lambda i, j: (i, 0)),
                  pl.BlockSpec((1, k1, tn), lambda i, j: (layer, 0, j)),
                  pl.BlockSpec((1, k2, tn), lambda i, j: (layer, 1, j))],
        out_specs=pl.BlockSpec((tm, tn), lambda i, j: (i, j)),
        out_shape=jax.ShapeDtypeStruct((m, n), out_dtype),
        compiler_params=_params(("parallel", "arbitrary")),
        name="matmul2",
    )(a1, a2, w, w)


def _ln_kernel(x_ref, m_ref, g_ref, b_ref, o_ref, ob_ref):
    h = DEEPNORM_ALPHA * x_ref[...] + m_ref[...]
    mu = jnp.mean(h, axis=-1, keepdims=True)
    d = h - mu
    var = jnp.mean(d * d, axis=-1, keepdims=True)
    y = d * lax.rsqrt(var + LN_EPS) * g_ref[...] + b_ref[...]
    o_ref[...] = y
    ob_ref[...] = y.astype(BF16)


def deepnorm_ln(x, mix, g, b, tm=128):
    t, d = x.shape
    tm = min(tm, t)
    row = pl.BlockSpec((tm, d), lambda i: (i, 0))
    vec = pl.BlockSpec((1, d), lambda i: (0, 0))
    return pl.pallas_call(
        _ln_kernel,
        grid=(t // tm,),
        in_specs=[row, row, vec, vec],
        out_specs=[row, row],
        out_shape=[jax.ShapeDtypeStruct((t, d), F32), jax.ShapeDtypeStruct((t, d), BF16)],
        compiler_params=_params(("parallel",)),
        name="deepnorm_ln",
    )(x, mix, g.reshape(1, d), b.reshape(1, d))


def _rms_kernel(x_ref, g_ref, o_ref):
    x = x_ref[...]
    y = x * lax.rsqrt(jnp.mean(x * x, axis=-1, keepdims=True) + RMS_EPS) * g_ref[...]
    o_ref[...] = y.astype(o_ref.dtype)


def rms_norm(x, g, tm=256):
    t, d = x.shape
    tm = min(tm, t)
    return pl.pallas_call(
        _rms_kernel,
        grid=(t // tm,),
        in_specs=[pl.BlockSpec((tm, d), lambda i: (i, 0)), pl.BlockSpec((1, d), lambda i: (0, 0))],
        out_specs=pl.BlockSpec((tm, d), lambda i: (i, 0)),
        out_shape=jax.ShapeDtypeStruct((t, d), BF16),
        compiler_params=_params(("parallel",)),
        name="rms_norm",
    )(x, g.reshape(1, d))


def _rope_kernel(x_ref, pos_ref, f_ref, o_ref):
    lane = lax.broadcasted_iota(I32, (1, LANES), 1)
    first = (lane % QK_ROPE) < (QK_ROPE // 2)
    ang = pos_ref[...].astype(F32) * f_ref[...]
    cos = jnp.cos(ang)
    sin = jnp.sin(ang)
    sin = jnp.where(first, -sin, sin)
    for c in range(x_ref.shape[1] // LANES):
        x = x_ref[:, c * LANES:(c + 1) * LANES]
        partner = jnp.where(first, pltpu.roll(x, LANES - QK_ROPE // 2, 1), pltpu.roll(x, QK_ROPE // 2, 1))
        o_ref[:, c * LANES:(c + 1) * LANES] = (x * cos + partner * sin).astype(o_ref.dtype)


def rope(x, pos_col, inv_freq_lanes, tm=256):
    t, w = x.shape
    tm = min(tm, t)
    return pl.pallas_call(
        _rope_kernel,
        grid=(t // tm,),
        in_specs=[pl.BlockSpec((tm, w), lambda i: (i, 0)),
                  pl.BlockSpec((tm, 1), lambda i: (i, 0)),
                  pl.BlockSpec((1, LANES), lambda i: (0, 0))],
        out_specs=pl.BlockSpec((tm, w), lambda i: (i, 0)),
        out_shape=jax.ShapeDtypeStruct((t, w), BF16),
        compiler_params=_params(("parallel",)),
        name="rope",
    )(x, pos_col, inv_freq_lanes)


def _sb_kernel(q_ref, k_ref, v_ref, o_ref, *, tile, group):
    i = pl.program_id(2)
    row = lax.broadcasted_iota(I32, (tile, tile), 0)
    col = lax.broadcasted_iota(I32, (tile, tile), 1)
    later = (row > col).astype(BF16)
    strict = col < row
    qs = [q_ref[:, h * HEAD_DIM:(h + 1) * HEAD_DIM] for h in range(group)]

    def step(j, state, diag):
        sl = pl.ds(pl.multiple_of(j * tile, tile), tile)
        heads = range(group)
        hs = [slice(h * HEAD_DIM, (h + 1) * HEAD_DIM) for h in heads]
        z = [lax.dot_general(qs[h], k_ref[sl, hs[h]], _NT, preferred_element_type=F32) for h in heads]
        u = [jnp.log2(1.0 + jnp.exp2(-jnp.abs(z[h]))) for h in heads]
        drop = [jnp.maximum(z[h], 0.0) + u[h] for h in heads]
        if diag:
            drop = [jnp.where(strict, d, 0.0) for d in drop]
        after = [jnp.dot(drop[h].astype(BF16), later, preferred_element_type=F32) for h in heads]
        w = [jnp.exp2(jnp.minimum(z[h], 0.0) - u[h] - after[h] - state[h][0]) for h in heads]
        if diag:
            w = [jnp.where(strict, x, 0.0) for x in w]
        pv = [jnp.dot(w[h].astype(BF16), v_ref[sl, hs[h]], preferred_element_type=F32) for h in heads]
        return tuple((state[h][0] + (after[h][:, 0:1] + drop[h][:, 0:1]), state[h][1] + pv[h]) for h in heads)

    init = tuple((jnp.zeros((tile, 1), F32), jnp.zeros((tile, HEAD_DIM), F32)) for _ in range(group))
    state = step(i, init, True)
    state = lax.fori_loop(0, i, lambda n, st: step(i - 1 - n, st, False), state)
    for h in range(group):
        o_ref[:, h * HEAD_DIM:(h + 1) * HEAD_DIM] = state[h][1].astype(o_ref.dtype)


def sb_attention(qkv, batch, tile=256, group=4):
    t = qkv.shape[0]
    s = t // batch
    h = qkv.shape[1] // (3 * HEAD_DIM)
    tile, group = min(tile, s), min(group, h)
    nq = s // tile
    ng = h // group
    w = group * HEAD_DIM
    return pl.pallas_call(
        functools.partial(_sb_kernel, tile=tile, group=group),
        grid=(batch, ng, nq),
        in_specs=[pl.BlockSpec((tile, w), lambda b, hh, i: (b * nq + i, hh)),
                  pl.BlockSpec((s, w), lambda b, hh, i: (b, ng + hh)),
                  pl.BlockSpec((s, w), lambda b, hh, i: (b, 2 * ng + hh))],
        out_specs=pl.BlockSpec((tile, w), lambda b, hh, i: (b * nq + i, hh)),
        out_shape=jax.ShapeDtypeStruct((t, h * HEAD_DIM), BF16),
        compiler_params=_params(("parallel", "parallel", "arbitrary")),
        name="sb_attention",
    )(qkv, qkv, qkv)


def _t5_bucket(dist):
    n = jnp.maximum(dist, 0)
    max_exact = N_BUCKETS // 2
    scaled = jnp.log(jnp.maximum(n, 1).astype(F32) / max_exact) / math.log(MAX_DISTANCE / max_exact)
    large = jnp.minimum(max_exact + (scaled * (N_BUCKETS - max_exact)).astype(I32), N_BUCKETS - 1)
    return jnp.where(n < max_exact, n, large)


def _dsa_kernel(far_ref,
                q_ref, k_ref, v_ref, qi_ref, wk_q_ref, wk_k_ref, posq_ref, posk_ref, rbt_ref, rbfar_ref,
                o_ref,
                keys_ref, wb_ref, qall_ref, m_ref, acc_ref, s_ref,
                *, tq, tk, ts, chunk, rb, nk, topk, n_heads, hc, par):
    b = pl.program_id(0)
    i = pl.program_id(1)
    nq = pl.num_programs(1)
    n_causal = (i + 1) * tq
    n_chunks = (n_causal + chunk - 1) // chunk

    qi = qi_ref[...].astype(BF16)
    for hh in range(N_IDX_HEADS):
        wb_ref[hh] = jnp.broadcast_to(wk_q_ref[:, IDX_DIM + hh:IDX_DIM + hh + 1], (tq, ts))

    def to_key(sc):
        sc = jnp.where(sc == 0.0, 0.0, sc)
        bits = pltpu.bitcast(sc, I32)
        return bits ^ ((bits >> 31) & 0x7FFFFFFF)

    neg_key = to_key(jnp.full((tq, ts), NEG_INF, F32))
    t_idx_s = i * tq + lax.broadcasted_iota(I32, (tq, ts), 0)
    col_s = lax.broadcasted_iota(I32, (tq, ts), 1)

    def score_tile(j, _):
        sl = pl.ds(pl.multiple_of(j * ts, ts), ts)

        @pl.when(j * ts < n_causal)
        def _():
            kj = wk_k_ref[sl, 0:IDX_DIM].astype(BF16)
            sc = jnp.zeros((tq, ts), F32)
            for hh in range(N_IDX_HEADS):
                lg = lax.dot_general(qi[:, hh * IDX_DIM:(hh + 1) * IDX_DIM], kj, _NT,
                                     preferred_element_type=F32)
                sc = sc + wb_ref[hh] * jnp.maximum(lg, 0.0)
            sc = sc * (N_IDX_HEADS ** -0.5)
            sc = jnp.where(j * ts + col_s <= t_idx_s, sc, NEG_INF)
            keys_ref[:, sl] = to_key(sc)

        @pl.when(j * ts >= n_causal)
        def _():
            keys_ref[:, sl] = neg_key

        return 0

    lax.fori_loop(0, n_chunks * (chunk // ts), score_tile, 0)

    n_rb = tq // rb

    def count(pred, ref_val):
        ref_b = [jnp.broadcast_to(ref_val[r * rb:(r + 1) * rb, :], (rb, LANES)) for r in range(n_rb)]

        def body(c, accs):
            accs = list(accs)
            for u in range(chunk // LANES):
                sl = pl.ds(pl.multiple_of(c * chunk + u * LANES, LANES), LANES)
                for r in range(n_rb):
                    hit = pred(keys_ref[r * rb:(r + 1) * rb, sl], ref_b[r])
                    accs[r] = accs[r] + jnp.where(hit, 1, 0)
            return tuple(accs)

        accs = lax.fori_loop(0, n_chunks, body, tuple(jnp.zeros((rb, LANES), I32) for _ in range(n_rb)))
        return jnp.concatenate([jnp.sum(a, axis=1, keepdims=True) for a in accs], axis=0)

    def bisect(it, off):
        trial = off | jnp.left_shift(jnp.int32(1), 31 - it)
        cand = trial + jnp.int32(INT_MIN)
        return jnp.where(count(lambda kk, c: kk >= c, cand) >= topk, trial, off)

    thr = lax.fori_loop(0, 32, bisect, jnp.zeros((tq, 1), I32)) + jnp.int32(INT_MIN)
    need = (topk - count(lambda kk, c: kk > c, thr)).astype(F32)

    for hh in range(n_heads):
        qall_ref[hh * tq:(hh + 1) * tq, :] = q_ref[:, hh * HEAD_DIM:(hh + 1) * HEAD_DIM]
    m_ref[...] = jnp.full(m_ref.shape, NEG_INF, F32)
    acc_ref[...] = jnp.zeros(acc_ref.shape, F32)
    r2 = lax.broadcasted_iota(I32, (tk, tk), 0)
    c2 = lax.broadcasted_iota(I32, (tk, tk), 1)
    earlier = (r2 < c2).astype(BF16)
    ones_col = (lax.broadcasted_iota(I32, (tk, LANES), 1) == 0).astype(BF16)
    pos_q = posq_ref[...]
    n_hc = n_heads // hc
    rows_c = hc * tq
    n_tiles = (n_causal + tk - 1) // tk
    t_idx = i * tq + lax.broadcasted_iota(I32, (tq, tk), 0)
    col = lax.broadcasted_iota(I32, (tq, tk), 1)

    def attend_tile(j, eq_before):
        sl = pl.ds(pl.multiple_of(j * tk, tk), tk)
        kt = keys_ref[:, sl]
        eq = jnp.where(kt == thr, 1.0, 0.0)
        rank = eq_before + jnp.dot(eq.astype(BF16), earlier, preferred_element_type=F32)
        take = jnp.where(kt > thr, 1.0, jnp.where(rank < need, eq, 0.0))
        mask = jnp.where(j * tk + col <= t_idx, take, 0.0) > 0.5
        is_far = far_ref[(b * nq + i) * nk + j] == 1
        k_t = k_ref[sl, :]
        v_ext = jnp.concatenate([v_ref[sl, :], ones_col], axis=1)

        def near_bias(c):
            bucket = _t5_bucket(pos_q - posk_ref[0, :, sl])
            per_head = []
            for hh in range(c * hc, (c + 1) * hc):
                tbl = jnp.broadcast_to(rbt_ref[hh:hh + 1, :], (tq, LANES))
                halves = [jnp.take_along_axis(tbl, bucket[:, x * LANES:(x + 1) * LANES], axis=1,
                                              mode="promise_in_bounds") for x in range(tk // LANES)]
                per_head.append(jnp.concatenate(halves, axis=1))
            return jnp.stack(per_head, axis=0)

        for g in range(n_hc // par):
            chunks = range(par)
            rows = [slice((g * par + c) * rows_c, (g * par + c + 1) * rows_c) for c in chunks]
            local = [slice(c * rows_c, (c + 1) * rows_c) for c in chunks]
            for c in chunks:
                s_ref[local[c], :] = lax.dot_general(qall_ref[rows[c], :], k_t, _NT, preferred_element_type=F32)

            @pl.when(jnp.logical_not(is_far))
            def _():
                for c in chunks:
                    s_ref[local[c], :] += near_bias(g * par + c).reshape(rows_c, tk)

            s = [jnp.where(mask[None], s_ref[local[c], :].reshape(hc, tq, tk), NEG_INF).reshape(rows_c, tk)
                 for c in chunks]
            shift = [jnp.where(is_far, rbfar_ref[rows[c], :], 0.0) for c in chunks]
            m_old = [m_ref[rows[c], :] for c in chunks]
            m_new = [jnp.maximum(m_old[c], jnp.max(s[c], axis=1, keepdims=True) + shift[c]) for c in chunks]
            p = [jnp.exp2(s[c] - (m_new[c] - shift[c])).astype(BF16) for c in chunks]
            pv = [jnp.dot(p[c], v_ext, preferred_element_type=F32) for c in chunks]
            for c in chunks:
                acc_ref[rows[c], :] = jnp.exp2(m_old[c] - m_new[c]) * acc_ref[rows[c], :] + pv[c]
                m_ref[rows[c], :] = m_new[c]
        return eq_before + jnp.sum(eq, axis=1, keepdims=True)

    lax.fori_loop(0, n_tiles, attend_tile, jnp.zeros((tq, 1), F32))
    for hh in range(n_heads):
        rows = slice(hh * tq, (hh + 1) * tq)
        o_ref[:, hh * HEAD_DIM:(hh + 1) * HEAD_DIM] = (
            acc_ref[rows, 0:HEAD_DIM] / acc_ref[rows, HEAD_DIM:HEAD_DIM + 1]).astype(o_ref.dtype)


def dsa_attention(qkv, idx, positions, rel_bias, batch, tq=256, tk=512, ts=256, chunk=1024, rb=128, hc=4,
                  par=2):
    t = qkv.shape[0]
    s = t // batch
    n_heads = (qkv.shape[1] - 2 * HEAD_DIM) // HEAD_DIM
    tq, tk, ts, chunk, rb = min(tq, s), min(tk, s), min(ts, s), min(chunk, s), min(rb, tq)
    assert chunk % tk == 0 and chunk % ts == 0 and s % chunk == 0 and tq % rb == 0
    hc = min(hc, n_heads)
    par = min(par, n_heads // hc)
    nq, nk = s // tq, s // tk
    topk = min(TOPK_MAX, s // 4)
    qi_w = N_IDX_HEADS * IDX_DIM
    pos_lo = positions.reshape(batch, nq, tq).min(axis=-1)
    pos_hi = positions.reshape(batch, nk, tk).max(axis=-1)
    far = (pos_lo[:, :, None] - pos_hi[:, None, :] >= MAX_DISTANCE).astype(I32).reshape(-1)
    rb2 = rel_bias.astype(F32) * LOG2E
    rbt = jnp.pad(rb2.T, ((0, 0), (0, LANES - N_BUCKETS)))
    rbfar = jnp.repeat(rb2[N_BUCKETS - 1], tq).reshape(n_heads * tq, 1)
    kernel = functools.partial(_dsa_kernel, tq=tq, tk=tk, ts=ts, chunk=chunk, rb=rb, nk=nk, topk=topk,
                               n_heads=n_heads, hc=hc, par=par)
    once = pl.Buffered(1)
    grid_spec = pltpu.PrefetchScalarGridSpec(
        num_scalar_prefetch=1,
        grid=(batch, nq),
        in_specs=[
            pl.BlockSpec((tq, n_heads * HEAD_DIM), lambda b, i, *_: (b * nq + i, 0)),
            pl.BlockSpec((s, HEAD_DIM), lambda b, i, *_: (b, n_heads), pipeline_mode=once),
            pl.BlockSpec((s, HEAD_DIM), lambda b, i, *_: (b, n_heads + 1), pipeline_mode=once),
            pl.BlockSpec((tq, qi_w), lambda b, i, *_: (b * nq + i, 0)),
            pl.BlockSpec((tq, LANES), lambda b, i, *_: (b * nq + i, qi_w // LANES)),
            pl.BlockSpec((s, LANES), lambda b, i, *_: (b, qi_w // LANES), pipeline_mode=once),
            pl.BlockSpec((tq, 1), lambda b, i, *_: (b * nq + i, 0)),
            pl.BlockSpec((1, 1, s), lambda b, i, *_: (b, 0, 0), pipeline_mode=once),
            pl.BlockSpec((n_heads, LANES), lambda b, i, *_: (0, 0), pipeline_mode=once),
            pl.BlockSpec((n_heads * tq, 1), lambda b, i, *_: (0, 0), pipeline_mode=once),
        ],
        out_specs=pl.BlockSpec((tq, n_heads * HEAD_DIM), lambda b, i, *_: (b * nq + i, 0)),
        scratch_shapes=[
            pltpu.VMEM((tq, s), I32),
            pltpu.VMEM((N_IDX_HEADS, tq, ts), F32),
            pltpu.VMEM((n_heads * tq, HEAD_DIM), BF16),
            pltpu.VMEM((n_heads * tq, 1), F32),
            pltpu.VMEM((n_heads * tq, 2 * HEAD_DIM), F32),
            pltpu.VMEM((par * hc * tq, tk), F32),
        ],
    )
    return pl.pallas_call(
        kernel,
        grid_spec=grid_spec,
        out_shape=jax.ShapeDtypeStruct((t, n_heads * HEAD_DIM), BF16),
        compiler_params=_params(("parallel", "arbitrary")),
        name="dsa_attention",
    )(far, qkv, qkv, qkv, idx, idx, idx, positions.reshape(t, 1), positions.reshape(batch, 1, s),
      rbt, rbfar)


def _mla_kernel(qn_ref, qr_ref, kn_ref, kr_ref, v_ref, o_ref, kcat_ref, vext_ref, *, tq, tkb, group):
    i = pl.program_id(2)
    wide = 2 * LANES

    @pl.when(i == 0)
    def _():
        ones_col = (lax.broadcasted_iota(I32, (kr_ref.shape[0], LANES), 1) == 0).astype(BF16)
        for h in range(group):
            kcat_ref[:, h * wide:h * wide + LANES] = kn_ref[:, h * LANES:(h + 1) * LANES]
            kcat_ref[:, h * wide + LANES:(h + 1) * wide] = kr_ref[...]
            vext_ref[:, h * wide:h * wide + LANES] = v_ref[:, h * LANES:(h + 1) * LANES]
            vext_ref[:, h * wide + LANES:(h + 1) * wide] = ones_col

    row = lax.broadcasted_iota(I32, (tq, tkb), 0)
    col = lax.broadcasted_iota(I32, (tq, tkb), 1)
    qs = [jnp.concatenate([qn_ref[:, h * LANES:(h + 1) * LANES], qr_ref[:, h * LANES:(h + 1) * LANES]], axis=1)
          for h in range(group)]
    n_full = (i * tq) // tkb

    def chunk(c, state, masked):
        sl = pl.ds(pl.multiple_of(c * tkb, tkb), tkb)
        heads = range(group)
        hs = [slice(h * wide, (h + 1) * wide) for h in heads]
        s = [lax.dot_general(qs[h], kcat_ref[sl, hs[h]], _NT, preferred_element_type=F32) for h in heads]
        if masked:
            s = [jnp.where(c * tkb + col <= i * tq + row, x, NEG_INF) for x in s]
        m_new = [jnp.maximum(state[h][0], jnp.max(s[h], axis=1, keepdims=True)) for h in heads]
        p = [jnp.exp2(s[h] - m_new[h]).astype(BF16) for h in heads]
        pv = [jnp.dot(p[h], vext_ref[sl, hs[h]], preferred_element_type=F32) for h in heads]
        return tuple((m_new[h], jnp.exp2(state[h][0] - m_new[h]) * state[h][1] + pv[h]) for h in heads)

    init = tuple((jnp.full((tq, 1), NEG_INF, F32), jnp.zeros((tq, 2 * LANES), F32)) for _ in range(group))
    state = lax.fori_loop(0, n_full, lambda c, st: chunk(c, st, False), init)
    state = chunk(n_full, state, True)
    for h in range(group):
        acc = state[h][1]
        o_ref[:, h * LANES:(h + 1) * LANES] = (acc[:, 0:V_DIM] / acc[:, V_DIM:V_DIM + 1]).astype(o_ref.dtype)


def mla_attention(qn, qr, kn, kr, v, batch, tq=512, tkb=1024, group=2):
    t = qn.shape[0]
    s = t // batch
    h = qn.shape[1] // QK_NOPE
    tq, tkb, group = min(tq, s), min(tkb, s), min(group, h)
    nq = s // tq
    w = group * LANES
    qspec = pl.BlockSpec((tq, w), lambda b, hh, i: (b * nq + i, hh))
    kspec = pl.BlockSpec((s, w), lambda b, hh, i: (b, hh))
    return pl.pallas_call(
        functools.partial(_mla_kernel, tq=tq, tkb=tkb, group=group),
        grid=(batch, h // group, nq),
        in_specs=[qspec, qspec, kspec, pl.BlockSpec((s, LANES), lambda b, hh, i: (b, 0)), kspec],
        out_specs=qspec,
        out_shape=jax.ShapeDtypeStruct((t, h * V_DIM), BF16),
        scratch_shapes=[pltpu.VMEM((s, 2 * w), BF16),
                        pltpu.VMEM((s, 2 * w), BF16)],
        compiler_params=_params(("parallel", "parallel", "arbitrary")),
        name="mla_attention",
    )(qn, qr, kn, kr, v)


def _first_max(vals):
    best = vals[0]
    for v in vals[1:]:
        best = jnp.maximum(best, v)
    idx = jnp.full(best.shape, len(vals) - 1, I32)
    for k in range(len(vals) - 2, -1, -1):
        idx = jnp.where(vals[k] == best, k, idx)
    return best, idx


def _router_kernel(x_ref, w_ref, b_ref, e_ref, wt_ref):
    x = x_ref[...]
    x_hi = x.astype(BF16)
    x_lo = (x - x_hi.astype(F32)).astype(BF16)
    wide = (jnp.dot(x_hi, w_ref[0], preferred_element_type=F32)
            + jnp.dot(x_hi, w_ref[1], preferred_element_type=F32)
            + jnp.dot(x_lo, w_ref[0], preferred_element_type=F32))
    logits = wide.T[0:N_EXPERTS, :]
    aff = jax.nn.sigmoid(logits)
    biased = aff + b_ref[...]
    a_rows = [aff[e:e + 1, :] for e in range(N_EXPERTS)]
    b_rows = [biased[e:e + 1, :] for e in range(N_EXPERTS)]
    g_score, g_e1, g_e2 = [], [], []
    for g in range(N_GROUPS):
        vals = b_rows[g * EXPERTS_PER_GROUP:(g + 1) * EXPERTS_PER_GROUP]
        top1, i1 = _first_max(vals)
        rest = [jnp.where(i1 == k, -jnp.inf, vals[k]) for k in range(EXPERTS_PER_GROUP)]
        top2, i2 = _first_max(rest)
        g_score.append(top1 + top2)
        g_e1.append(i1 + g * EXPERTS_PER_GROUP)
        g_e2.append(i2 + g * EXPERTS_PER_GROUP)
    _, grp = _first_max(g_score)
    e1, e2 = g_e1[N_GROUPS - 1], g_e2[N_GROUPS - 1]
    for g in range(N_GROUPS - 2, -1, -1):
        e1 = jnp.where(grp == g, g_e1[g], e1)
        e2 = jnp.where(grp == g, g_e2[g], e2)
    w1 = jnp.zeros_like(a_rows[0])
    w2 = jnp.zeros_like(a_rows[0])
    for e in range(N_EXPERTS):
        w1 = jnp.where(e1 == e, a_rows[e], w1)
        w2 = jnp.where(e2 == e, a_rows[e], w2)
    tot = w1 + w2
    e_ref[0:1, :] = e1
    e_ref[1:2, :] = e2
    wt_ref[0:1, :] = w1 / tot
    wt_ref[1:2, :] = w2 / tot


def router(x, w_router, router_bias, tm=512):
    t, d = x.shape
    tm = min(tm, t)
    w = jnp.pad(w_router.astype(F32), ((0, 0), (0, LANES - N_EXPERTS)))
    w_hi = w.astype(BF16)
    w_split = jnp.stack([w_hi, (w - w_hi.astype(F32)).astype(BF16)])
    return pl.pallas_call(
        _router_kernel,
        grid=(t // tm,),
        in_specs=[pl.BlockSpec((tm, d), lambda i: (i, 0)),
                  pl.BlockSpec((2, d, LANES), lambda i: (0, 0, 0)),
                  pl.BlockSpec((N_EXPERTS, 1), lambda i: (0, 0))],
        out_specs=[pl.BlockSpec((TOP_K_EXPERTS, tm), lambda i: (0, i)),
                   pl.BlockSpec((TOP_K_EXPERTS, tm), lambda i: (0, i))],
        out_shape=[jax.ShapeDtypeStruct((TOP_K_EXPERTS, t), I32),
                   jax.ShapeDtypeStruct((TOP_K_EXPERTS, t), F32)],
        compiler_params=_params(("parallel",)),
        name="router",
    )(x, w_split, router_bias.reshape(N_EXPERTS, 1))


def dispatch_plan(experts, tm):
    k, t = experts.shape
    n_slots = k * t
    n_rows = n_slots + N_EXPERTS * tm
    n_tiles = n_rows // tm
    flat = experts.reshape(n_slots)
    order = jnp.argsort(flat, stable=True).astype(I32)
    inverse = jnp.argsort(order).astype(I32)
    ids = jnp.arange(N_EXPERTS, dtype=I32)
    counts = jnp.sum((flat[:, None] == ids[None, :]).astype(I32), axis=0)
    padded = (counts + tm - 1) // tm * tm
    pad_end = jnp.cumsum(padded)
    pad_start = pad_end - padded
    start = jnp.cumsum(counts) - counts
    dest = pad_start[flat] + inverse - start[flat]
    tile_start = jnp.arange(n_tiles, dtype=I32) * tm
    tile_expert = jnp.minimum(jnp.sum((tile_start[:, None] >= pad_end[None, :]).astype(I32), axis=1),
                              N_EXPERTS - 1)
    row = jnp.arange(n_rows, dtype=I32)
    row_expert = jnp.repeat(tile_expert, tm)
    rank = row - pad_start[row_expert]
    sorted_pos = jnp.clip(start[row_expert] + rank, 0, n_slots - 1)
    row_token = jnp.where(rank < counts[row_expert], order[sorted_pos] % t, 0)
    tile_active = (tile_start < pad_end[-1]).astype(I32)
    return row_token.reshape(n_tiles, 1, tm), dest.reshape(k, t), tile_expert, tile_active


def _ffn_kernel(exp_ref, act_ref, idx_ref, idx_next_ref, x_hbm, wg_ref, wu_ref, wd_ref, o_ref,
                land, xs, sem, *, tm):
    t = pl.program_id(0)
    last = pl.num_programs(0) - 1

    def row_copy(ids, r):
        return pltpu.make_async_copy(x_hbm.at[pl.ds(ids[0, 0, r], 1), :], land.at[pl.ds(r, 1), :], sem)

    def start_tile(ids):
        def body(r, c):
            row_copy(ids, r).start()
            return c
        lax.fori_loop(0, tm, body, 0, unroll=8)

    def wait_tile(ids):
        def body(r, c):
            row_copy(ids, r).wait()
            return c
        lax.fori_loop(0, tm, body, 0, unroll=8)

    @pl.when(jnp.logical_and(t == 0, act_ref[0] == 1))
    def _():
        start_tile(idx_ref)

    @pl.when(act_ref[t] == 1)
    def _():
        wait_tile(idx_ref)
        xs[...] = land[...].astype(BF16)

    @pl.when(jnp.logical_and(t < last, act_ref[jnp.minimum(t + 1, last)] == 1))
    def _():
        start_tile(idx_next_ref)

    @pl.when(act_ref[t] == 1)
    def _():
        x = xs[...]
        h = jax.nn.silu(jnp.dot(x, wg_ref[0], preferred_element_type=F32)) * jnp.dot(
            x, wu_ref[0], preferred_element_type=F32)
        o_ref[...] = jnp.dot(h.astype(BF16), wd_ref[0], preferred_element_type=F32)

    @pl.when(act_ref[t] == 0)
    def _():
        o_ref[...] = jnp.zeros_like(o_ref)


def expert_ffn(x, row_token, tile_expert, tile_active, w_gate, w_up, w_down):
    n_tiles, _, tm = row_token.shape
    d = x.shape[1]
    f = w_gate.shape[2]
    ids = lambda shift: pl.BlockSpec(
        (1, 1, tm), lambda t, ex, ac: (jnp.minimum(t + shift, n_tiles - 1), 0, 0), memory_space=pltpu.SMEM)
    grid_spec = pltpu.PrefetchScalarGridSpec(
        num_scalar_prefetch=2,
        grid=(n_tiles,),
        in_specs=[ids(0), ids(1),
                  pl.BlockSpec(memory_space=pl.ANY),
                  pl.BlockSpec((1, d, f), lambda t, ex, ac: (ex[t], 0, 0)),
                  pl.BlockSpec((1, d, f), lambda t, ex, ac: (ex[t], 0, 0)),
                  pl.BlockSpec((1, f, d), lambda t, ex, ac: (ex[t], 0, 0))],
        out_specs=pl.BlockSpec((tm, d), lambda t, ex, ac: (t, 0)),
        scratch_shapes=[pltpu.VMEM((tm, d), F32),
                        pltpu.VMEM((tm, d), BF16),
                        pltpu.SemaphoreType.DMA(())],
    )
    return pl.pallas_call(
        functools.partial(_ffn_kernel, tm=tm),
        grid_spec=grid_spec,
        out_shape=jax.ShapeDtypeStruct((n_tiles * tm, d), F32),
        compiler_params=_params(("arbitrary",)),
        name="expert_ffn",
    )(tile_expert, tile_active, row_token, row_token, x, w_gate, w_up, w_down)


def _combine_ln_kernel(dest_ref, dest_next_ref, x_ref, w_ref, g_ref, b_ref, y_hbm, o_ref, ob_ref, buf, sem,
                       *, tm):
    i = pl.program_id(0)
    last = pl.num_programs(0) - 1
    slot = i % 2

    def row_copy(ids, k, r, s):
        return pltpu.make_async_copy(y_hbm.at[pl.ds(ids[0, k, r], 1), :], buf.at[s, k, pl.ds(r, 1), :],
                                     sem.at[s])

    def start_tile(ids, s):
        def body(r, c):
            for k in range(TOP_K_EXPERTS):
                row_copy(ids, k, r, s).start()
            return c
        lax.fori_loop(0, tm, body, 0, unroll=8)

    @pl.when(i == 0)
    def _():
        start_tile(dest_ref, 0)

    @pl.when(i < last)
    def _():
        start_tile(dest_next_ref, 1 - slot)

    def wait(r, c):
        for k in range(TOP_K_EXPERTS):
            row_copy(dest_ref, k, r, slot).wait()
        return c

    lax.fori_loop(0, tm, wait, 0, unroll=8)
    h = DEEPNORM_ALPHA * x_ref[...] + (w_ref[:, 0:1] * buf[slot, 0] + w_ref[:, 1:2] * buf[slot, 1])
    mu = jnp.mean(h, axis=-1, keepdims=True)
    dlt = h - mu
    var = jnp.mean(dlt * dlt, axis=-1, keepdims=True)
    y = dlt * lax.rsqrt(var + LN_EPS) * g_ref[...] + b_ref[...]
    o_ref[...] = y
    ob_ref[...] = y.astype(BF16)


def combine_ln(x, ys, dest, weights, g, b, tm=256):
    t, d = x.shape
    tm = min(tm, t)
    k = dest.shape[0]
    n = t // tm
    dest_tiles = dest.reshape(k, n, tm).transpose(1, 0, 2)
    row = pl.BlockSpec((tm, d), lambda i: (i, 0))
    vec = pl.BlockSpec((1, d), lambda i: (0, 0))
    ids = lambda shift: pl.BlockSpec((1, k, tm), lambda i: (jnp.minimum(i + shift, n - 1), 0, 0),
                                     memory_space=pltpu.SMEM)
    return pl.pallas_call(
        functools.partial(_combine_ln_kernel, tm=tm),
        grid=(n,),
        in_specs=[ids(0), ids(1),
                  row,
                  pl.BlockSpec((tm, k), lambda i: (i, 0)),
                  vec, vec,
                  pl.BlockSpec(memory_space=pl.ANY)],
        out_specs=[row, row],
        out_shape=[jax.ShapeDtypeStruct((t, d), F32), jax.ShapeDtypeStruct((t, d), BF16)],
        scratch_shapes=[pltpu.VMEM((2, k, tm, d), F32), pltpu.SemaphoreType.DMA((2,))],
        compiler_params=_params(("arbitrary",)),
        name="combine_ln",
    )(dest_tiles, dest_tiles, x, weights.T, g.reshape(1, d), b.reshape(1, d), ys)


def even_mixer(xb, positions, w_in_all, w_out_all, layer, rel_bias, batch):
    q_scale = HEAD_DIM ** -0.5 * LOG2E
    o2, o3, o5 = 3 * SB_W, 3 * SB_W + DSA_W, 3 * SB_W + DSA_W + 2 * HEAD_DIM
    o6 = o5 + N_IDX_HEADS * IDX_DIM
    o7 = o6 + N_IDX_HEADS
    scale_sb = jnp.where(jnp.arange(o2) < SB_W, q_scale, 1.0)
    scale_ds = jnp.where(jnp.arange(o5 - o2) < DSA_W, q_scale, 1.0)
    qkv_sb = matmul_w32(xb, w_in_all, layer, 0, o2, scale_sb, BF16, 1024, 512)
    qkv_ds = matmul_w32(xb, w_in_all, layer, o2, o5 - o2, scale_ds, BF16, 1024, 256)
    w_in = w_in_all[layer, :, o5:]
    pad = jnp.zeros((w_in.shape[0], LANES - IDX_DIM - N_IDX_HEADS), w_in.dtype)
    w_ix = jnp.concatenate([w_in[:, :o6 - o5] * IDX_DIM ** -0.5, w_in[:, o7 - o5:], w_in[:, o6 - o5:o7 - o5], pad],
                           axis=1).astype(BF16)
    idx = matmul(xb, w_ix, F32, 1024, w_ix.shape[1])
    o_sb = sb_attention(qkv_sb, batch)
    o_ds = dsa_attention(qkv_ds, idx, positions, rel_bias, batch)
    return matmul2(o_sb, o_ds, w_out_all, layer, F32, 1024, 512)


def mla_mixer(xb, positions, w_down_all, layer, g_q, g_kv, w_uq, w_ukv, w_o_all, batch):
    t = xb.shape[0]
    scale = (QK_NOPE + QK_ROPE) ** -0.5 * LOG2E
    w_kr = w_down_all[layer, :, Q_LORA + KV_LORA:].astype(BF16)
    kr_pad = jnp.zeros((w_kr.shape[0], LANES - QK_ROPE), BF16)
    c_q = matmul_w32(xb, w_down_all, layer, 0, Q_LORA, jnp.ones((Q_LORA,)), F32, 1024, 512)
    c_kv = matmul_w32(xb, w_down_all, layer, Q_LORA, KV_LORA, jnp.ones((KV_LORA,)), F32, 1024, 512)
    k_rope = matmul(xb, jnp.concatenate([w_kr, kr_pad], axis=1), F32, 1024, LANES)
    c_q = rms_norm(c_q, g_q)
    c_kv = rms_norm(c_kv, g_kv)
    w_uq = (w_uq * scale).reshape(Q_LORA, N_HEADS_MLA, QK_NOPE + QK_ROPE)
    w_qn = w_uq[:, :, :QK_NOPE].reshape(Q_LORA, N_HEADS_MLA * QK_NOPE).astype(BF16)
    w_qr = jnp.pad(w_uq[:, :, QK_NOPE:], ((0, 0), (0, 0), (0, LANES - QK_ROPE)))
    w_qr = w_qr.reshape(Q_LORA, N_HEADS_MLA * LANES).astype(BF16)
    w_ukv = w_ukv.reshape(KV_LORA, N_HEADS_MLA, QK_NOPE + V_DIM)
    w_kn = w_ukv[:, :, :QK_NOPE].reshape(KV_LORA, N_HEADS_MLA * QK_NOPE).astype(BF16)
    w_v = w_ukv[:, :, QK_NOPE:].reshape(KV_LORA, N_HEADS_MLA * V_DIM).astype(BF16)
    qn = matmul(c_q, w_qn, BF16, 1024, 1024)
    qr = matmul(c_q, w_qr, F32, 1024, 1024)
    kn = matmul(c_kv, w_kn, BF16, 1024, 1024)
    v = matmul(c_kv, w_v, BF16, 1024, 1024)
    half = QK_ROPE // 2
    inv_freq = ROPE_THETA ** (-jnp.arange(half, dtype=F32) / half)
    inv_freq = jnp.tile(inv_freq, LANES // half).reshape(1, LANES)
    pos_col = positions.reshape(t, 1)
    qr = rope(qr, pos_col, inv_freq)
    kr = rope(k_rope, pos_col, inv_freq)
    o = mla_attention(qn, qr, kn, kr, v, batch)
    d_out = w_o_all.shape[2]
    return matmul_w32(o, w_o_all, layer, 0, d_out, jnp.ones((d_out,)), F32, 1024, 512)


def moe_ln(x, w_router, router_bias, w_gate, w_up, w_down, g, b, tm=256):
    experts, weights = router(x, w_router, router_bias)
    row_token, dest, tile_expert, tile_active = dispatch_plan(experts, tm)
    ys = expert_ffn(x, row_token, tile_expert, tile_active, w_gate.astype(BF16), w_up.astype(BF16),
                    w_down.astype(BF16))
    return combine_ln(x, ys, dest, weights, g, b)


def kernel(x, positions, rel_bias, even_w_in, even_w_out, mla_w_down, mla_g_q, mla_g_kv, mla_w_uq,
           mla_w_ukv, mla_w_o, w_router, router_bias, exp_w_gate, exp_w_up, exp_w_down, ln_g, ln_b):
    batch, seq, d = x.shape
    x = x.reshape(batch * seq, d)
    xb = x.astype(BF16)
    for layer in range(DEPTH):
        i = layer // 2
        if layer % 2 == 0:
            mix = even_mixer(xb, positions, even_w_in, even_w_out, i, rel_bias, batch)
        else:
            mix = mla_mixer(xb, positions, mla_w_down, i, mla_g_q[i], mla_g_kv[i], mla_w_uq[i],
                            mla_w_ukv[i], mla_w_o, batch)
        x, xb = deepnorm_ln(x, mix, ln_g[layer, 0], ln_b[layer, 0])
        x, xb = moe_ln(x, w_router, router_bias, exp_w_gate[layer], exp_w_up[layer], exp_w_down[layer],
                       ln_g[layer, 1], ln_b[layer, 1])
    return x.reshape(batch, seq, d)
```

```python
import functools
import math

import jax
import jax.numpy as jnp
from jax import lax
from jax.experimental import pallas as pl
from jax.experimental.pallas import tpu as pltpu

F32 = jnp.float32
BF16 = jnp.bfloat16
I32 = jnp.int32

D_MODEL = 4096
DEPTH = 4
HEAD_DIM = 128
N_HEADS_SB = 16
N_HEADS_DSA = 16
N_IDX_HEADS = 8
IDX_DIM = 64
TOPK_MAX = 256
N_BUCKETS = 32
MAX_DISTANCE = 128
N_HEADS_MLA = 32
Q_LORA = 1024
KV_LORA = 512
QK_NOPE = 128
QK_ROPE = 64
V_DIM = 128
ROPE_THETA = 10000.0
N_EXPERTS = 16
N_GROUPS = 4
EXPERTS_PER_GROUP = N_EXPERTS // N_GROUPS
TOP_K_EXPERTS = 2
D_FF_EXPERT = 768
DEEPNORM_ALPHA = (2 * DEPTH) ** 0.25
LN_EPS = 1e-5
RMS_EPS = 1e-6
NEG_INF = -1e30
LOG2E = math.log2(math.e)
SB_W = N_HEADS_SB * HEAD_DIM
DSA_W = N_HEADS_DSA * HEAD_DIM

LANES = 128
INT_MIN = -(2 ** 31)
VMEM_LIMIT = 56 * 1024 * 1024

_NT = (((1,), (1,)), ((), ()))


def _params(sem, vmem=VMEM_LIMIT):
    return pltpu.CompilerParams(dimension_semantics=sem, vmem_limit_bytes=vmem)


def _mm_kernel(a_ref, b_ref, o_ref):
    o_ref[...] = jnp.dot(a_ref[...], b_ref[...], preferred_element_type=F32).astype(o_ref.dtype)


def _mm2_kernel(a1_ref, a2_ref, b1_ref, b2_ref, o_ref):
    acc = jnp.dot(a1_ref[...], b1_ref[0].astype(BF16), preferred_element_type=F32)
    acc = acc + jnp.dot(a2_ref[...], b2_ref[0].astype(BF16), preferred_element_type=F32)
    o_ref[...] = acc.astype(o_ref.dtype)


def matmul(a, b, out_dtype, tm, tn):
    m, k = a.shape
    n = b.shape[1]
    tm, tn = min(tm, m), min(tn, n)
    assert m % tm == 0 and n % tn == 0
    return pl.pallas_call(
        _mm_kernel,
        grid=(m // tm, n // tn),
        in_specs=[pl.BlockSpec((tm, k), lambda i, j: (i, 0)),
                  pl.BlockSpec((k, tn), lambda i, j: (0, j))],
        out_specs=pl.BlockSpec((tm, tn), lambda i, j: (i, j)),
        out_shape=jax.ShapeDtypeStruct((m, n), out_dtype),
        compiler_params=_params(("parallel", "arbitrary")),
        name="matmul",
    )(a, b)


def _mm_w32_kernel(a_ref, b_ref, s_ref, o_ref):
    b = (b_ref[0] * s_ref[...]).astype(BF16)
    o_ref[...] = jnp.dot(a_ref[...], b, preferred_element_type=F32).astype(o_ref.dtype)


def matmul_w32(a, w, layer, col0, n, col_scale, out_dtype, tm, tn):
    m, k = a.shape
    tm = min(tm, m)
    assert m % tm == 0 and n % tn == 0 and col0 % tn == 0
    j0 = col0 // tn
    return pl.pallas_call(
        _mm_w32_kernel,
        grid=(m // tm, n // tn),
        in_specs=[pl.BlockSpec((tm, k), lambda i, j: (i, 0)),
                  pl.BlockSpec((1, k, tn), lambda i, j: (layer, 0, j0 + j)),
                  pl.BlockSpec((1, tn), lambda i, j: (0, j))],
        out_specs=pl.BlockSpec((tm, tn), lambda i, j: (i, j)),
        out_shape=jax.ShapeDtypeStruct((m, n), out_dtype),
        compiler_params=_params(("parallel", "arbitrary")),
        name="matmul_w32",
    )(a, w, col_scale.reshape(1, n).astype(F32))


def matmul2(a1, a2, w, layer, out_dtype, tm, tn):
    m, k1 = a1.shape
    k2 = a2.shape[1]
    n = w.shape[2]
    tm, tn = min(tm, m), min(tn, n)
    assert m % tm == 0 and n % tn == 0 and k1 == k2 and w.shape[1] == k1 + k2
    return pl.pallas_call(
        _mm2_kernel,
        grid=(m // tm, n // tn),
        in_specs=[pl.BlockSpec((tm, k1), lambda i, j: (i, 0)),
                  pl.BlockSpec((tm, k2), lambda i, j: (i, 0)),
                  pl.BlockSpec((1, k1, tn), lambda i, j: (layer, 0, j)),
                  pl.BlockSpec((1, k2, tn), lambda i, j: (layer, 1, j))],
        out_specs=pl.BlockSpec((tm, tn), lambda i, j: (i, j)),
        out_shape=jax.ShapeDtypeStruct((m, n), out_dtype),
        compiler_params=_params(("parallel", "arbitrary")),
        name="matmul2",
    )(a1, a2, w, w)


def _ln_kernel(x_ref, m_ref, g_ref, b_ref, o_ref, ob_ref):
    h = DEEPNORM_ALPHA * x_ref[...] + m_ref[...]
    mu = jnp.mean(h, axis=-1, keepdims=True)
    d = h - mu
    var = jnp.mean(d * d, axis=-1, keepdims=True)
    y = d * lax.rsqrt(var + LN_EPS) * g_ref[...] + b_ref[...]
    o_ref[...] = y
    ob_ref[...] = y.astype(BF16)


def deepnorm_ln(x, mix, g, b, tm=128):
    t, d = x.shape
    tm = min(tm, t)
    row = pl.BlockSpec((tm, d), lambda i: (i, 0))
    vec = pl.BlockSpec((1, d), lambda i: (0, 0))
    return pl.pallas_call(
        _ln_kernel,
        grid=(t // tm,),
        in_specs=[row, row, vec, vec],
        out_specs=[row, row],
        out_shape=[jax.ShapeDtypeStruct((t, d), F32), jax.ShapeDtypeStruct((t, d), BF16)],
        compiler_params=_params(("parallel",)),
        name="deepnorm_ln",
    )(x, mix, g.reshape(1, d), b.reshape(1, d))


def _rms_kernel(x_ref, g_ref, o_ref):
    x = x_ref[...]
    y = x * lax.rsqrt(jnp.mean(x * x, axis=-1, keepdims=True) + RMS_EPS) * g_ref[...]
    o_ref[...] = y.astype(o_ref.dtype)


def rms_norm(x, g, tm=256):
    t, d = x.shape
    tm = min(tm, t)
    return pl.pallas_call(
        _rms_kernel,
        grid=(t // tm,),
        in_specs=[pl.BlockSpec((tm, d), lambda i: (i, 0)), pl.BlockSpec((1, d), lambda i: (0, 0))],
        out_specs=pl.BlockSpec((tm, d), lambda i: (i, 0)),
        out_shape=jax.ShapeDtypeStruct((t, d), BF16),
        compiler_params=_params(("parallel",)),
        name="rms_norm",
    )(x, g.reshape(1, d))


def _rope_kernel(x_ref, pos_ref, f_ref, o_ref):
    lane = lax.broadcasted_iota(I32, (1, LANES), 1)
    first = (lane % QK_ROPE) < (QK_ROPE // 2)
    ang = pos_ref[...].astype(F32) * f_ref[...]
    cos = jnp.cos(ang)
    sin = jnp.sin(ang)
    sin = jnp.where(first, -sin, sin)
    for c in range(x_ref.shape[1] // LANES):
        x = x_ref[:, c * LANES:(c + 1) * LANES]
        partner = jnp.where(first, pltpu.roll(x, LANES - QK_ROPE // 2, 1), pltpu.roll(x, QK_ROPE // 2, 1))
        o_ref[:, c * LANES:(c + 1) * LANES] = (x * cos + partner * sin).astype(o_ref.dtype)


def rope(x, pos_col, inv_freq_lanes, tm=256):
    t, w = x.shape
    tm = min(tm, t)
    return pl.pallas_call(
        _rope_kernel,
        grid=(t // tm,),
        in_specs=[pl.BlockSpec((tm, w), lambda i: (i, 0)),
                  pl.BlockSpec((tm, 1), lambda i: (i, 0)),
                  pl.BlockSpec((1, LANES), lambda i: (0, 0))],
        out_specs=pl.BlockSpec((tm, w), lambda i: (i, 0)),
        out_shape=jax.ShapeDtypeStruct((t, w), BF16),
        compiler_params=_params(("parallel",)),
        name="rope",
    )(x, pos_col, inv_freq_lanes)


def _sb_kernel(q_ref, k_ref, v_ref, o_ref, *, tile, group):
    i = pl.program_id(2)
    row = lax.broadcasted_iota(I32, (tile, tile), 0)
    col = lax.broadcasted_iota(I32, (tile, tile), 1)
    later = (row > col).astype(BF16)
    strict = col < row
    qs = [q_ref[:, h * HEAD_DIM:(h + 1) * HEAD_DIM] for h in range(group)]

    def step(j, state, diag):
        sl = pl.ds(pl.multiple_of(j * tile, tile), tile)
        heads = range(group)
        hs = [slice(h * HEAD_DIM, (h + 1) * HEAD_DIM) for h in heads]
        z = [lax.dot_general(qs[h], k_ref[sl, hs[h]], _NT, preferred_element_type=F32) for h in heads]
        u = [jnp.log2(1.0 + jnp.exp2(-jnp.abs(z[h]))) for h in heads]
        log_sig = [jnp.minimum(z[h], 0.0) - u[h] for h in heads]
        drop = [z[h] - log_sig[h] for h in heads]
        if diag:
            drop = [jnp.where(strict, d, 0.0) for d in drop]
        after = [jnp.dot(drop[h].astype(BF16), later, preferred_element_type=F32) for h in heads]
        w = [jnp.exp2(log_sig[h] - after[h] - state[h][0]) for h in heads]
        if diag:
            w = [jnp.where(strict, x, 0.0) for x in w]
        pv = [jnp.dot(w[h].astype(BF16), v_ref[sl, hs[h]], preferred_element_type=F32) for h in heads]
        return tuple((state[h][0] + (after[h][:, 0:1] + drop[h][:, 0:1]), state[h][1] + pv[h]) for h in heads)

    init = tuple((jnp.zeros((tile, 1), F32), jnp.zeros((tile, HEAD_DIM), F32)) for _ in range(group))
    state = step(i, init, True)
    state = lax.fori_loop(0, i, lambda n, st: step(i - 1 - n, st, False), state)
    for h in range(group):
        o_ref[:, h * HEAD_DIM:(h + 1) * HEAD_DIM] = state[h][1].astype(o_ref.dtype)


def sb_attention(qkv, batch, tile=256, group=4):
    t = qkv.shape[0]
    s = t // batch
    h = qkv.shape[1] // (3 * HEAD_DIM)
    tile, group = min(tile, s), min(group, h)
    nq = s // tile
    ng = h // group
    w = group * HEAD_DIM
    return pl.pallas_call(
        functools.partial(_sb_kernel, tile=tile, group=group),
        grid=(batch, ng, nq),
        in_specs=[pl.BlockSpec((tile, w), lambda b, hh, i: (b * nq + i, hh)),
                  pl.BlockSpec((s, w), lambda b, hh, i: (b, ng + hh)),
                  pl.BlockSpec((s, w), lambda b, hh, i: (b, 2 * ng + hh))],
        out_specs=pl.BlockSpec((tile, w), lambda b, hh, i: (b * nq + i, hh)),
        out_shape=jax.ShapeDtypeStruct((t, h * HEAD_DIM), BF16),
        compiler_params=_params(("parallel", "parallel", "arbitrary")),
        name="sb_attention",
    )(qkv, qkv, qkv)


def _t5_bucket(dist):
    n = jnp.maximum(dist, 0)
    max_exact = N_BUCKETS // 2
    scaled = jnp.log(jnp.maximum(n, 1).astype(F32) / max_exact) / math.log(MAX_DISTANCE / max_exact)
    large = jnp.minimum(max_exact + (scaled * (N_BUCKETS - max_exact)).astype(I32), N_BUCKETS - 1)
    return jnp.where(n < max_exact, n, large)


def _dsa_kernel(far_ref,
                q_ref, k_ref, v_ref, qi_ref, wk_q_ref, wk_k_ref, posq_ref, posk_ref, rbt_ref, rbfar_ref,
                o_ref,
                keys_ref, wb_ref, qall_ref, m_ref, acc_ref, s_ref,
                *, tq, tk, ts, chunk, rb, nk, topk, n_heads, hc, par):
    b = pl.program_id(0)
    i = pl.program_id(1)
    nq = pl.num_programs(1)
    n_causal = (i + 1) * tq
    n_chunks = (n_causal + chunk - 1) // chunk

    qi = qi_ref[...].astype(BF16)
    for hh in range(N_IDX_HEADS):
        wb_ref[hh] = jnp.broadcast_to(wk_q_ref[:, IDX_DIM + hh:IDX_DIM + hh + 1], (tq, ts))

    def to_key(sc):
        sc = jnp.where(sc == 0.0, 0.0, sc)
        bits = pltpu.bitcast(sc, I32)
        return bits ^ ((bits >> 31) & 0x7FFFFFFF)

    neg_key = to_key(jnp.full((tq, ts), NEG_INF, F32))
    t_idx_s = i * tq + lax.broadcasted_iota(I32, (tq, ts), 0)
    col_s = lax.broadcasted_iota(I32, (tq, ts), 1)

    def score_tile(j, _):
        sl = pl.ds(pl.multiple_of(j * ts, ts), ts)

        @pl.when(j * ts < n_causal)
        def _():
            kj = wk_k_ref[sl, 0:IDX_DIM].astype(BF16)
            sc = jnp.zeros((tq, ts), F32)
            for hh in range(N_IDX_HEADS):
                lg = lax.dot_general(qi[:, hh * IDX_DIM:(hh + 1) * IDX_DIM], kj, _NT,
                                     preferred_element_type=F32)
                sc = sc + wb_ref[hh] * jnp.maximum(lg, 0.0)
            sc = sc * (N_IDX_HEADS ** -0.5)
            sc = jnp.where(j * ts + col_s <= t_idx_s, sc, NEG_INF)
            keys_ref[:, sl] = to_key(sc)

        @pl.when(j * ts >= n_causal)
        def _():
            keys_ref[:, sl] = neg_key

        return 0

    lax.fori_loop(0, n_chunks * (chunk // ts), score_tile, 0)

    n_rb = tq // rb

    def count(pred, ref_val):
        ref_b = [jnp.broadcast_to(ref_val[r * rb:(r + 1) * rb, :], (rb, LANES)) for r in range(n_rb)]

        def body(c, accs):
            accs = list(accs)
            for u in range(chunk // LANES):
                sl = pl.ds(pl.multiple_of(c * chunk + u * LANES, LANES), LANES)
                for r in range(n_rb):
                    hit = pred(keys_ref[r * rb:(r + 1) * rb, sl], ref_b[r])
                    accs[r] = accs[r] + jnp.where(hit, 1, 0)
            return tuple(accs)

        accs = lax.fori_loop(0, n_chunks, body, tuple(jnp.zeros((rb, LANES), I32) for _ in range(n_rb)))
        return jnp.concatenate([jnp.sum(a, axis=1, keepdims=True) for a in accs], axis=0)

    def bisect(it, off):
        trial = off | jnp.left_shift(jnp.int32(1), 31 - it)
        cand = trial + jnp.int32(INT_MIN)
        return jnp.where(count(lambda kk, c: kk >= c, cand) >= topk, trial, off)

    thr = lax.fori_loop(0, 32, bisect, jnp.zeros((tq, 1), I32)) + jnp.int32(INT_MIN)
    need = (topk - count(lambda kk, c: kk > c, thr)).astype(F32)

    for hh in range(n_heads):
        qall_ref[hh * tq:(hh + 1) * tq, :] = q_ref[:, hh * HEAD_DIM:(hh + 1) * HEAD_DIM]
    m_ref[...] = jnp.full(m_ref.shape, NEG_INF, F32)
    acc_ref[...] = jnp.zeros(acc_ref.shape, F32)
    r2 = lax.broadcasted_iota(I32, (tk, tk), 0)
    c2 = lax.broadcasted_iota(I32, (tk, tk), 1)
    earlier = (r2 < c2).astype(BF16)
    ones_col = (lax.broadcasted_iota(I32, (tk, LANES), 1) == 0).astype(BF16)
    pos_q = posq_ref[...]
    n_hc = n_heads // hc
    rows_c = hc * tq
    n_tiles = (n_causal + tk - 1) // tk
    t_idx = i * tq + lax.broadcasted_iota(I32, (tq, tk), 0)
    col = lax.broadcasted_iota(I32, (tq, tk), 1)

    def attend_tile(j, eq_before):
        sl = pl.ds(pl.multiple_of(j * tk, tk), tk)
        kt = keys_ref[:, sl]
        eq = jnp.where(kt == thr, 1.0, 0.0)
        rank = eq_before + jnp.dot(eq.astype(BF16), earlier, preferred_element_type=F32)
        take = jnp.where(kt > thr, 1.0, jnp.where(rank < need, eq, 0.0))
        mask = jnp.where(j * tk + col <= t_idx, take, 0.0) > 0.5
        is_far = far_ref[(b * nq + i) * nk + j] == 1
        k_t = k_ref[sl, :]
        v_ext = jnp.concatenate([v_ref[sl, :], ones_col], axis=1)

        def near_bias(c):
            bucket = _t5_bucket(pos_q - posk_ref[0, :, sl])
            per_head = []
            for hh in range(c * hc, (c + 1) * hc):
                tbl = jnp.broadcast_to(rbt_ref[hh:hh + 1, :], (tq, LANES))
                halves = [jnp.take_along_axis(tbl, bucket[:, x * LANES:(x + 1) * LANES], axis=1,
                                              mode="promise_in_bounds") for x in range(tk // LANES)]
                per_head.append(jnp.concatenate(halves, axis=1))
            return jnp.stack(per_head, axis=0)

        for g in range(n_hc // par):
            chunks = range(par)
            rows = [slice((g * par + c) * rows_c, (g * par + c + 1) * rows_c) for c in chunks]
            local = [slice(c * rows_c, (c + 1) * rows_c) for c in chunks]
            for c in chunks:
                s_ref[local[c], :] = lax.dot_general(qall_ref[rows[c], :], k_t, _NT, preferred_element_type=F32)

            @pl.when(jnp.logical_not(is_far))
            def _():
                for c in chunks:
                    s_ref[local[c], :] += near_bias(g * par + c).reshape(rows_c, tk)

            s = [jnp.where(mask[None], s_ref[local[c], :].reshape(hc, tq, tk), NEG_INF).reshape(rows_c, tk)
                 for c in chunks]
            shift = [jnp.where(is_far, rbfar_ref[rows[c], :], 0.0) for c in chunks]
            m_old = [m_ref[rows[c], :] for c in chunks]
            m_new = [jnp.maximum(m_old[c], jnp.max(s[c], axis=1, keepdims=True) + shift[c]) for c in chunks]
            p = [jnp.exp2(s[c] - (m_new[c] - shift[c])).astype(BF16) for c in chunks]
            pv = [jnp.dot(p[c], v_ext, preferred_element_type=F32) for c in chunks]
            for c in chunks:
                acc_ref[rows[c], :] = jnp.exp2(m_old[c] - m_new[c]) * acc_ref[rows[c], :] + pv[c]
                m_ref[rows[c], :] = m_new[c]
        return eq_before + jnp.sum(eq, axis=1, keepdims=True)

    lax.fori_loop(0, n_tiles, attend_tile, jnp.zeros((tq, 1), F32))
    for hh in range(n_heads):
        rows = slice(hh * tq, (hh + 1) * tq)
        o_ref[:, hh * HEAD_DIM:(hh + 1) * HEAD_DIM] = (
            acc_ref[rows, 0:HEAD_DIM] / acc_ref[rows, HEAD_DIM:HEAD_DIM + 1]).astype(o_ref.dtype)


def dsa_attention(qkv, idx, positions, rel_bias, batch, tq=256, tk=512, ts=512, chunk=1024, rb=128, hc=4,
                  par=2):
    t = qkv.shape[0]
    s = t // batch
    n_heads = (qkv.shape[1] - 2 * HEAD_DIM) // HEAD_DIM
    tq, tk, ts, chunk, rb = min(tq, s), min(tk, s), min(ts, s), min(chunk, s), min(rb, tq)
    assert chunk % tk == 0 and chunk % ts == 0 and s % chunk == 0 and tq % rb == 0
    hc = min(hc, n_heads)
    par = min(par, n_heads // hc)
    nq, nk = s // tq, s // tk
    topk = min(TOPK_MAX, s // 4)
    qi_w = N_IDX_HEADS * IDX_DIM
    pos_lo = positions.reshape(batch, nq, tq).min(axis=-1)
    pos_hi = positions.reshape(batch, nk, tk).max(axis=-1)
    far = (pos_lo[:, :, None] - pos_hi[:, None, :] >= MAX_DISTANCE).astype(I32).reshape(-1)
    rb2 = rel_bias.astype(F32) * LOG2E
    rbt = jnp.pad(rb2.T, ((0, 0), (0, LANES - N_BUCKETS)))
    rbfar = jnp.repeat(rb2[N_BUCKETS - 1], tq).reshape(n_heads * tq, 1)
    kernel = functools.partial(_dsa_kernel, tq=tq, tk=tk, ts=ts, chunk=chunk, rb=rb, nk=nk, topk=topk,
                               n_heads=n_heads, hc=hc, par=par)
    once = pl.Buffered(1)
    grid_spec = pltpu.PrefetchScalarGridSpec(
        num_scalar_prefetch=1,
        grid=(batch, nq),
        in_specs=[
            pl.BlockSpec((tq, n_heads * HEAD_DIM), lambda b, i, *_: (b * nq + i, 0)),
            pl.BlockSpec((s, HEAD_DIM), lambda b, i, *_: (b, n_heads), pipeline_mode=once),
            pl.BlockSpec((s, HEAD_DIM), lambda b, i, *_: (b, n_heads + 1), pipeline_mode=once),
            pl.BlockSpec((tq, qi_w), lambda b, i, *_: (b * nq + i, 0)),
            pl.BlockSpec((tq, LANES), lambda b, i, *_: (b * nq + i, qi_w // LANES)),
            pl.BlockSpec((s, LANES), lambda b, i, *_: (b, qi_w // LANES), pipeline_mode=once),
            pl.BlockSpec((tq, 1), lambda b, i, *_: (b * nq + i, 0)),
            pl.BlockSpec((1, 1, s), lambda b, i, *_: (b, 0, 0), pipeline_mode=once),
            pl.BlockSpec((n_heads, LANES), lambda b, i, *_: (0, 0), pipeline_mode=once),
            pl.BlockSpec((n_heads * tq, 1), lambda b, i, *_: (0, 0), pipeline_mode=once),
        ],
        out_specs=pl.BlockSpec((tq, n_heads * HEAD_DIM), lambda b, i, *_: (b * nq + i, 0)),
        scratch_shapes=[
            pltpu.VMEM((tq, s), I32),
            pltpu.VMEM((N_IDX_HEADS, tq, ts), F32),
            pltpu.VMEM((n_heads * tq, HEAD_DIM), BF16),
            pltpu.VMEM((n_heads * tq, 1), F32),
            pltpu.VMEM((n_heads * tq, 2 * HEAD_DIM), F32),
            pltpu.VMEM((par * hc * tq, tk), F32),
        ],
    )
    return pl.pallas_call(
        kernel,
        grid_spec=grid_spec,
        out_shape=jax.ShapeDtypeStruct((t, n_heads * HEAD_DIM), BF16),
        compiler_params=_params(("parallel", "arbitrary")),
        name="dsa_attention",
    )(far, qkv, qkv, qkv, idx, idx, idx, positions.reshape(t, 1), positions.reshape(batch, 1, s),
      rbt, rbfar)


def _mla_kernel(qn_ref, qr_ref, kn_ref, kr_ref, v_ref, o_ref, kcat_ref, vext_ref, *, tq, tkb, group):
    i = pl.program_id(2)
    wide = 2 * LANES

    @pl.when(i == 0)
    def _():
        ones_col = (lax.broadcasted_iota(I32, (kr_ref.shape[0], LANES), 1) == 0).astype(BF16)
        for h in range(group):
            kcat_ref[:, h * wide:h * wide + LANES] = kn_ref[:, h * LANES:(h + 1) * LANES]
            kcat_ref[:, h * wide + LANES:(h + 1) * wide] = kr_ref[...]
            vext_ref[:, h * wide:h * wide + LANES] = v_ref[:, h * LANES:(h + 1) * LANES]
            vext_ref[:, h * wide + LANES:(h + 1) * wide] = ones_col

    row = lax.broadcasted_iota(I32, (tq, tkb), 0)
    col = lax.broadcasted_iota(I32, (tq, tkb), 1)
    qs = [jnp.concatenate([qn_ref[:, h * LANES:(h + 1) * LANES], qr_ref[:, h * LANES:(h + 1) * LANES]], axis=1)
          for h in range(group)]
    n_full = (i * tq) // tkb

    def chunk(c, state, masked):
        sl = pl.ds(pl.multiple_of(c * tkb, tkb), tkb)
        heads = range(group)
        hs = [slice(h * wide, (h + 1) * wide) for h in heads]
        s = [lax.dot_general(qs[h], kcat_ref[sl, hs[h]], _NT, preferred_element_type=F32) for h in heads]
        if masked:
            s = [jnp.where(c * tkb + col <= i * tq + row, x, NEG_INF) for x in s]
        m_new = [jnp.maximum(state[h][0], jnp.max(s[h], axis=1, keepdims=True)) for h in heads]
        p = [jnp.exp2(s[h] - m_new[h]).astype(BF16) for h in heads]
        pv = [jnp.dot(p[h], vext_ref[sl, hs[h]], preferred_element_type=F32) for h in heads]
        return tuple((m_new[h], jnp.exp2(state[h][0] - m_new[h]) * state[h][1] + pv[h]) for h in heads)

    init = tuple((jnp.full((tq, 1), NEG_INF, F32), jnp.zeros((tq, 2 * LANES), F32)) for _ in range(group))
    state = lax.fori_loop(0, n_full, lambda c, st: chunk(c, st, False), init)
    state = chunk(n_full, state, True)
    for h in range(group):
        acc = state[h][1]
        o_ref[:, h * LANES:(h + 1) * LANES] = (acc[:, 0:V_DIM] / acc[:, V_DIM:V_DIM + 1]).astype(o_ref.dtype)


def mla_attention(qn, qr, kn, kr, v, batch, tq=512, tkb=1024, group=2):
    t = qn.shape[0]
    s = t // batch
    h = qn.shape[1] // QK_NOPE
    tq, tkb, group = min(tq, s), min(tkb, s), min(group, h)
    nq = s // tq
    w = group * LANES
    qspec = pl.BlockSpec((tq, w), lambda b, hh, i: (b * nq + i, hh))
    kspec = pl.BlockSpec((s, w), lambda b, hh, i: (b, hh))
    return pl.pallas_call(
        functools.partial(_mla_kernel, tq=tq, tkb=tkb, group=group),
        grid=(batch, h // group, nq),
        in_specs=[qspec, qspec, kspec, pl.BlockSpec((s, LANES), lambda b, hh, i: (b, 0)), kspec],
        out_specs=qspec,
        out_shape=jax.ShapeDtypeStruct((t, h * V_DIM), BF16),
        scratch_shapes=[pltpu.VMEM((s, 2 * w), BF16),
                        pltpu.VMEM((s, 2 * w), BF16)],
        compiler_params=_params(("parallel", "parallel", "arbitrary")),
        name="mla_attention",
    )(qn, qr, kn, kr, v)


def _first_max(vals):
    best = vals[0]
    for v in vals[1:]:
        best = jnp.maximum(best, v)
    idx = jnp.full(best.shape, len(vals) - 1, I32)
    for k in range(len(vals) - 2, -1, -1):
        idx = jnp.where(vals[k] == best, k, idx)
    return best, idx


def _router_kernel(x_ref, w_ref, b_ref, e_ref, wt_ref):
    x = x_ref[...]
    x_hi = x.astype(BF16)
    x_lo = (x - x_hi.astype(F32)).astype(BF16)
    wide = (jnp.dot(x_hi, w_ref[0], preferred_element_type=F32)
            + jnp.dot(x_hi, w_ref[1], preferred_element_type=F32)
            + jnp.dot(x_lo, w_ref[0], preferred_element_type=F32))
    logits = wide.T[0:N_EXPERTS, :]
    aff = jax.nn.sigmoid(logits)
    biased = aff + b_ref[...]
    a_rows = [aff[e:e + 1, :] for e in range(N_EXPERTS)]
    b_rows = [biased[e:e + 1, :] for e in range(N_EXPERTS)]
    g_score, g_e1, g_e2 = [], [], []
    for g in range(N_GROUPS):
        vals = b_rows[g * EXPERTS_PER_GROUP:(g + 1) * EXPERTS_PER_GROUP]
        top1, i1 = _first_max(vals)
        rest = [jnp.where(i1 == k, -jnp.inf, vals[k]) for k in range(EXPERTS_PER_GROUP)]
        top2, i2 = _first_max(rest)
        g_score.append(top1 + top2)
        g_e1.append(i1 + g * EXPERTS_PER_GROUP)
        g_e2.append(i2 + g * EXPERTS_PER_GROUP)
    _, grp = _first_max(g_score)
    e1, e2 = g_e1[N_GROUPS - 1], g_e2[N_GROUPS - 1]
    for g in range(N_GROUPS - 2, -1, -1):
        e1 = jnp.where(grp == g, g_e1[g], e1)
        e2 = jnp.where(grp == g, g_e2[g], e2)
    w1 = jnp.zeros_like(a_rows[0])
    w2 = jnp.zeros_like(a_rows[0])
    for e in range(N_EXPERTS):
        w1 = jnp.where(e1 == e, a_rows[e], w1)
        w2 = jnp.where(e2 == e, a_rows[e], w2)
    tot = w1 + w2
    e_ref[0:1, :] = e1
    e_ref[1:2, :] = e2
    wt_ref[0:1, :] = w1 / tot
    wt_ref[1:2, :] = w2 / tot


def router(x, w_router, router_bias, tm=512):
    t, d = x.shape
    tm = min(tm, t)
    w = jnp.pad(w_router.astype(F32), ((0, 0), (0, LANES - N_EXPERTS)))
    w_hi = w.astype(BF16)
    w_split = jnp.stack([w_hi, (w - w_hi.astype(F32)).astype(BF16)])
    return pl.pallas_call(
        _router_kernel,
        grid=(t // tm,),
        in_specs=[pl.BlockSpec((tm, d), lambda i: (i, 0)),
                  pl.BlockSpec((2, d, LANES), lambda i: (0, 0, 0)),
                  pl.BlockSpec((N_EXPERTS, 1), lambda i: (0, 0))],
        out_specs=[pl.BlockSpec((TOP_K_EXPERTS, tm), lambda i: (0, i)),
                   pl.BlockSpec((TOP_K_EXPERTS, tm), lambda i: (0, i))],
        out_shape=[jax.ShapeDtypeStruct((TOP_K_EXPERTS, t), I32),
                   jax.ShapeDtypeStruct((TOP_K_EXPERTS, t), F32)],
        compiler_params=_params(("parallel",)),
        name="router",
    )(x, w_split, router_bias.reshape(N_EXPERTS, 1))


def dispatch_plan(experts, tm):
    k, t = experts.shape
    n_slots = k * t
    n_rows = n_slots + N_EXPERTS * tm
    n_tiles = n_rows // tm
    flat = experts.reshape(n_slots)
    order = jnp.argsort(flat, stable=True).astype(I32)
    inverse = jnp.argsort(order).astype(I32)
    ids = jnp.arange(N_EXPERTS, dtype=I32)
    counts = jnp.sum((flat[:, None] == ids[None, :]).astype(I32), axis=0)
    padded = (counts + tm - 1) // tm * tm
    pad_end = jnp.cumsum(padded)
    pad_start = pad_end - padded
    start = jnp.cumsum(counts) - counts
    dest = pad_start[flat] + inverse - start[flat]
    tile_start = jnp.arange(n_tiles, dtype=I32) * tm
    tile_expert = jnp.minimum(jnp.sum((tile_start[:, None] >= pad_end[None, :]).astype(I32), axis=1),
                              N_EXPERTS - 1)
    row = jnp.arange(n_rows, dtype=I32)
    row_expert = jnp.repeat(tile_expert, tm)
    rank = row - pad_start[row_expert]
    sorted_pos = jnp.clip(start[row_expert] + rank, 0, n_slots - 1)
    row_token = jnp.where(rank < counts[row_expert], order[sorted_pos] % t, 0)
    tile_active = (tile_start < pad_end[-1]).astype(I32)
    return row_token.reshape(n_tiles, 1, tm), dest.reshape(k, t), tile_expert, tile_active


def _ffn_kernel(exp_ref, act_ref, idx_ref, idx_next_ref, x_hbm, wg_ref, wu_ref, wd_ref, o_ref,
                land, xs, sem, *, tm):
    t = pl.program_id(0)
    last = pl.num_programs(0) - 1

    def row_copy(ids, r):
        return pltpu.make_async_copy(x_hbm.at[pl.ds(ids[0, 0, r], 1), :], land.at[pl.ds(r, 1), :], sem)

    def start_tile(ids):
        def body(r, c):
            row_copy(ids, r).start()
            return c
        lax.fori_loop(0, tm, body, 0, unroll=8)

    def wait_tile(ids):
        def body(r, c):
            row_copy(ids, r).wait()
            return c
        lax.fori_loop(0, tm, body, 0, unroll=8)

    @pl.when(jnp.logical_and(t == 0, act_ref[0] == 1))
    def _():
        start_tile(idx_ref)

    @pl.when(act_ref[t] == 1)
    def _():
        wait_tile(idx_ref)
        xs[...] = land[...].astype(BF16)

    @pl.when(jnp.logical_and(t < last, act_ref[jnp.minimum(t + 1, last)] == 1))
    def _():
        start_tile(idx_next_ref)

    @pl.when(act_ref[t] == 1)
    def _():
        x = xs[...]
        h = jax.nn.silu(jnp.dot(x, wg_ref[0], preferred_element_type=F32)) * jnp.dot(
            x, wu_ref[0], preferred_element_type=F32)
        o_ref[...] = jnp.dot(h.astype(BF16), wd_ref[0], preferred_element_type=F32)

    @pl.when(act_ref[t] == 0)
    def _():
        o_ref[...] = jnp.zeros_like(o_ref)


def expert_ffn(x, row_token, tile_expert, tile_active, w_gate, w_up, w_down):
    n_tiles, _, tm = row_token.shape
    d = x.shape[1]
    f = w_gate.shape[2]
    ids = lambda shift: pl.BlockSpec(
        (1, 1, tm), lambda t, ex, ac: (jnp.minimum(t + shift, n_tiles - 1), 0, 0), memory_space=pltpu.SMEM)
    grid_spec = pltpu.PrefetchScalarGridSpec(
        num_scalar_prefetch=2,
        grid=(n_tiles,),
        in_specs=[ids(0), ids(1),
                  pl.BlockSpec(memory_space=pl.ANY),
                  pl.BlockSpec((1, d, f), lambda t, ex, ac: (ex[t], 0, 0)),
                  pl.BlockSpec((1, d, f), lambda t, ex, ac: (ex[t], 0, 0)),
                  pl.BlockSpec((1, f, d), lambda t, ex, ac: (ex[t], 0, 0))],
        out_specs=pl.BlockSpec((tm, d), lambda t, ex, ac: (t, 0)),
        scratch_shapes=[pltpu.VMEM((tm, d), F32),
                        pltpu.VMEM((tm, d), BF16),
                        pltpu.SemaphoreType.DMA(())],
    )
    return pl.pallas_call(
        functools.partial(_ffn_kernel, tm=tm),
        grid_spec=grid_spec,
        out_shape=jax.ShapeDtypeStruct((n_tiles * tm, d), F32),
        compiler_params=_params(("arbitrary",)),
        name="expert_ffn",
    )(tile_expert, tile_active, row_token, row_token, x, w_gate, w_up, w_down)


def _combine_ln_kernel(dest_ref, dest_next_ref, x_ref, w_ref, g_ref, b_ref, y_hbm, o_ref, ob_ref, buf, sem,
                       *, tm):
    i = pl.program_id(0)
    last = pl.num_programs(0) - 1
    slot = i % 2

    def row_copy(ids, k, r, s):
        return pltpu.make_async_copy(y_hbm.at[pl.ds(ids[0, k, r], 1), :], buf.at[s, k, pl.ds(r, 1), :],
                                     sem.at[s])

    def start_tile(ids, s):
        def body(r, c):
            for k in range(TOP_K_EXPERTS):
                row_copy(ids, k, r, s).start()
            return c
        lax.fori_loop(0, tm, body, 0, unroll=8)

    @pl.when(i == 0)
    def _():
        start_tile(dest_ref, 0)

    @pl.when(i < last)
    def _():
        start_tile(dest_next_ref, 1 - slot)

    def wait(r, c):
        for k in range(TOP_K_EXPERTS):
            row_copy(dest_ref, k, r, slot).wait()
        return c

    lax.fori_loop(0, tm, wait, 0, unroll=8)
    h = DEEPNORM_ALPHA * x_ref[...] + (w_ref[:, 0:1] * buf[slot, 0] + w_ref[:, 1:2] * buf[slot, 1])
    mu = jnp.mean(h, axis=-1, keepdims=True)
    dlt = h - mu
    var = jnp.mean(dlt * dlt, axis=-1, keepdims=True)
    y = dlt * lax.rsqrt(var + LN_EPS) * g_ref[...] + b_ref[...]
    o_ref[...] = y
    ob_ref[...] = y.astype(BF16)


def combine_ln(x, ys, dest, weights, g, b, tm=256):
    t, d = x.shape
    tm = min(tm, t)
    k = dest.shape[0]
    n = t // tm
    dest_tiles = dest.reshape(k, n, tm).transpose(1, 0, 2)
    row = pl.BlockSpec((tm, d), lambda i: (i, 0))
    vec = pl.BlockSpec((1, d), lambda i: (0, 0))
    ids = lambda shift: pl.BlockSpec((1, k, tm), lambda i: (jnp.minimum(i + shift, n - 1), 0, 0),
                                     memory_space=pltpu.SMEM)
    return pl.pallas_call(
        functools.partial(_combine_ln_kernel, tm=tm),
        grid=(n,),
        in_specs=[ids(0), ids(1),
                  row,
                  pl.BlockSpec((tm, k), lambda i: (i, 0)),
                  vec, vec,
                  pl.BlockSpec(memory_space=pl.ANY)],
        out_specs=[row, row],
        out_shape=[jax.ShapeDtypeStruct((t, d), F32), jax.ShapeDtypeStruct((t, d), BF16)],
        scratch_shapes=[pltpu.VMEM((2, k, tm, d), F32), pltpu.SemaphoreType.DMA((2,))],
        compiler_params=_params(("arbitrary",)),
        name="combine_ln",
    )(dest_tiles, dest_tiles, x, weights.T, g.reshape(1, d), b.reshape(1, d), ys)


def even_mixer(xb, positions, w_in_all, w_out_all, layer, rel_bias, batch):
    q_scale = HEAD_DIM ** -0.5 * LOG2E
    o2, o3, o5 = 3 * SB_W, 3 * SB_W + DSA_W, 3 * SB_W + DSA_W + 2 * HEAD_DIM
    o6 = o5 + N_IDX_HEADS * IDX_DIM
    o7 = o6 + N_IDX_HEADS
    scale_sb = jnp.where(jnp.arange(o2) < SB_W, q_scale, 1.0)
    scale_ds = jnp.where(jnp.arange(o5 - o2) < DSA_W, q_scale, 1.0)
    qkv_sb = matmul_w32(xb, w_in_all, layer, 0, o2, scale_sb, BF16, 1024, 512)
    qkv_ds = matmul_w32(xb, w_in_all, layer, o2, o5 - o2, scale_ds, BF16, 1024, 256)
    w_in = w_in_all[layer, :, o5:]
    pad = jnp.zeros((w_in.shape[0], LANES - IDX_DIM - N_IDX_HEADS), w_in.dtype)
    w_ix = jnp.concatenate([w_in[:, :o6 - o5] * IDX_DIM ** -0.5, w_in[:, o7 - o5:], w_in[:, o6 - o5:o7 - o5], pad],
                           axis=1).astype(BF16)
    idx = matmul(xb, w_ix, F32, 1024, w_ix.shape[1])
    o_sb = sb_attention(qkv_sb, batch)
    o_ds = dsa_attention(qkv_ds, idx, positions, rel_bias, batch)
    return matmul2(o_sb, o_ds, w_out_all, layer, F32, 1024, 512)


def mla_mixer(xb, positions, w_down_all, layer, g_q, g_kv, w_uq, w_ukv, w_o_all, batch):
    t = xb.shape[0]
    scale = (QK_NOPE + QK_ROPE) ** -0.5 * LOG2E
    w_kr = w_down_all[layer, :, Q_LORA + KV_LORA:].astype(BF16)
    kr_pad = jnp.zeros((w_kr.shape[0], LANES - QK_ROPE), BF16)
    c_q = matmul_w32(xb, w_down_all, layer, 0, Q_LORA, jnp.ones((Q_LORA,)), F32, 1024, 512)
    c_kv = matmul_w32(xb, w_down_all, layer, Q_LORA, KV_LORA, jnp.ones((KV_LORA,)), F32, 1024, 512)
    k_rope = matmul(xb, jnp.concatenate([w_kr, kr_pad], axis=1), F32, 1024, LANES)
    c_q = rms_norm(c_q, g_q)
    c_kv = rms_norm(c_kv, g_kv)
    w_uq = (w_uq * scale).reshape(Q_LORA, N_HEADS_MLA, QK_NOPE + QK_ROPE)
    w_qn = w_uq[:, :, :QK_NOPE].reshape(Q_LORA, N_HEADS_MLA * QK_NOPE).astype(BF16)
    w_qr = jnp.pad(w_uq[:, :, QK_NOPE:], ((0, 0), (0, 0), (0, LANES - QK_ROPE)))
    w_qr = w_qr.reshape(Q_LORA, N_HEADS_MLA * LANES).astype(BF16)
    w_ukv = w_ukv.reshape(KV_LORA, N_HEADS_MLA, QK_NOPE + V_DIM)
    w_kn = w_ukv[:, :, :QK_NOPE].reshape(KV_LORA, N_HEADS_MLA * QK_NOPE).astype(BF16)
    w_v = w_ukv[:, :, QK_NOPE:].reshape(KV_LORA, N_HEADS_MLA * V_DIM).astype(BF16)
    qn = matmul(c_q, w_qn, BF16, 1024, 1024)
    qr = matmul(c_q, w_qr, F32, 1024, 1024)
    kn = matmul(c_kv, w_kn, BF16, 1024, 1024)
    v = matmul(c_kv, w_v, BF16, 1024, 1024)
    half = QK_ROPE // 2
    inv_freq = ROPE_THETA ** (-jnp.arange(half, dtype=F32) / half)
    inv_freq = jnp.tile(inv_freq, LANES // half).reshape(1, LANES)
    pos_col = positions.reshape(t, 1)
    qr = rope(qr, pos_col, inv_freq)
    kr = rope(k_rope, pos_col, inv_freq)
    o = mla_attention(qn, qr, kn, kr, v, batch)
    d_out = w_o_all.shape[2]
    return matmul_w32(o, w_o_all, layer, 0, d_out, jnp.ones((d_out,)), F32, 1024, 512)


def moe_ln(x, w_router, router_bias, w_gate, w_up, w_down, g, b, tm=256):
    experts, weights = router(x, w_router, router_bias)
    row_token, dest, tile_expert, tile_active = dispatch_plan(experts, tm)
    ys = expert_ffn(x, row_token, tile_expert, tile_active, w_gate.astype(BF16), w_up.astype(BF16),
                    w_down.astype(BF16))
    return combine_ln(x, ys, dest, weights, g, b)


def kernel(x, positions, rel_bias, even_w_in, even_w_out, mla_w_down, mla_g_q, mla_g_kv, mla_w_uq,
           mla_w_ukv, mla_w_o, w_router, router_bias, exp_w_gate, exp_w_up, exp_w_down, ln_g, ln_b):
    batch, seq, d = x.shape
    x = x.reshape(batch * seq, d)
    xb = x.astype(BF16)
    for layer in range(DEPTH):
        i = layer // 2
        if layer % 2 == 0:
            mix = even_mixer(xb, positions, even_w_in, even_w_out, i, rel_bias, batch)
        else:
            mix = mla_mixer(xb, positions, mla_w_down, i, mla_g_q[i], mla_g_kv[i], mla_w_uq[i],
                            mla_w_ukv[i], mla_w_o, batch)
        x, xb = deepnorm_ln(x, mix, ln_g[layer, 0], ln_b[layer, 0])
        x, xb = moe_ln(x, w_router, router_bias, exp_w_gate[layer], exp_w_up[layer], exp_w_down[layer],
                       ln_g[layer, 1], ln_b[layer, 1])
    return x.reshape(batch, seq, d)
```

```python
import functools
import math

import jax
import jax.numpy as jnp
from jax import lax
from jax.experimental import pallas as pl
from jax.experimental.pallas import tpu as pltpu

F32 = jnp.float32
BF16 = jnp.bfloat16
I32 = jnp.int32

D_MODEL = 4096
DEPTH = 4
HEAD_DIM = 128
N_HEADS_SB = 16
N_HEADS_DSA = 16
N_IDX_HEADS = 8
IDX_DIM = 64
TOPK_MAX = 256
N_BUCKETS = 32
MAX_DISTANCE = 128
N_HEADS_MLA = 32
Q_LORA = 1024
KV_LORA = 512
QK_NOPE = 128
QK_ROPE = 64
V_DIM = 128
ROPE_THETA = 10000.0
N_EXPERTS = 16
N_GROUPS = 4
EXPERTS_PER_GROUP = N_EXPERTS // N_GROUPS
TOP_K_EXPERTS = 2
D_FF_EXPERT = 768
DEEPNORM_ALPHA = (2 * DEPTH) ** 0.25
LN_EPS = 1e-5
RMS_EPS = 1e-6
NEG_INF = -1e30
LOG2E = math.log2(math.e)
SB_W = N_HEADS_SB * HEAD_DIM
DSA_W = N_HEADS_DSA * HEAD_DIM

LANES = 128
INT_MIN = -(2 ** 31)
VMEM_LIMIT = 56 * 1024 * 1024

_NT = (((1,), (1,)), ((), ()))


def _params(sem, vmem=VMEM_LIMIT):
    return pltpu.CompilerParams(dimension_semantics=sem, vmem_limit_bytes=vmem)


def _mm_kernel(a_ref, b_ref, o_ref):
    o_ref[...] = jnp.dot(a_ref[...], b_ref[...], preferred_element_type=F32).astype(o_ref.dtype)


def _mm2_kernel(a1_ref, a2_ref, b1_ref, b2_ref, o_ref):
    acc = jnp.dot(a1_ref[...], b1_ref[0].astype(BF16), preferred_element_type=F32)
    acc = acc + jnp.dot(a2_ref[...], b2_ref[0].astype(BF16), preferred_element_type=F32)
    o_ref[...] = acc.astype(o_ref.dtype)


def matmul(a, b, out_dtype, tm, tn):
    m, k = a.shape
    n = b.shape[1]
    tm, tn = min(tm, m), min(tn, n)
    assert m % tm == 0 and n % tn == 0
    return pl.pallas_call(
        _mm_kernel,
        grid=(m // tm, n // tn),
        in_specs=[pl.BlockSpec((tm, k), lambda i, j: (i, 0)),
                  pl.BlockSpec((k, tn), lambda i, j: (0, j))],
        out_specs=pl.BlockSpec((tm, tn), lambda i, j: (i, j)),
        out_shape=jax.ShapeDtypeStruct((m, n), out_dtype),
        compiler_params=_params(("parallel", "arbitrary")),
        name="matmul",
    )(a, b)


def _mm_w32_kernel(a_ref, b_ref, s_ref, o_ref):
    b = (b_ref[0] * s_ref[...]).astype(BF16)
    o_ref[...] = jnp.dot(a_ref[...], b, preferred_element_type=F32).astype(o_ref.dtype)


def matmul_w32(a, w, layer, col0, n, col_scale, out_dtype, tm, tn):
    m, k = a.shape
    tm = min(tm, m)
    assert m % tm == 0 and n % tn == 0 and col0 % tn == 0
    j0 = col0 // tn
    return pl.pallas_call(
        _mm_w32_kernel,
        grid=(m // tm, n // tn),
        in_specs=[pl.BlockSpec((tm, k), lambda i, j: (i, 0)),
                  pl.BlockSpec((1, k, tn), lambda i, j: (layer, 0, j0 + j)),
                  pl.BlockSpec((1, tn), lambda i, j: (0, j))],
        out_specs=pl.BlockSpec((tm, tn), lambda i, j: (i, j)),
        out_shape=jax.ShapeDtypeStruct((m, n), out_dtype),
        compiler_params=_params(("parallel", "arbitrary")),
        name="matmul_w32",
    )(a, w, col_scale.reshape(1, n).astype(F32))


def matmul2(a1, a2, w, layer, out_dtype, tm, tn):
    m, k1 = a1.shape
    k2 = a2.shape[1]
    n = w.shape[2]
    tm, tn = min(tm, m), min(tn, n)
    assert m % tm == 0 and n % tn == 0 and k1 == k2 and w.shape[1] == k1 + k2
    return pl.pallas_call(
        _mm2_kernel,
        grid=(m // tm, n // tn),
        in_specs=[pl.BlockSpec((tm, k1), lambda i, j: (i, 0)),
                  pl.BlockSpec((tm, k2), lambda i, j: (i, 0)),
                  pl.BlockSpec((1, k1, tn), lambda i, j: (layer, 0, j)),
                  pl.BlockSpec((1, k2, tn), lambda i, j: (layer, 1, j))],
        out_specs=pl.BlockSpec((tm, tn), lambda i, j: (i, j)),
        out_shape=jax.ShapeDtypeStruct((m, n), out_dtype),
        compiler_params=_params(("parallel", "arbitrary")),
        name="matmul2",
    )(a1, a2, w, w)


def _ln_kernel(x_ref, m_ref, g_ref, b_ref, o_ref, ob_ref):
    h = DEEPNORM_ALPHA * x_ref[...] + m_ref[...]
    mu = jnp.mean(h, axis=-1, keepdims=True)
    d = h - mu
    var = jnp.mean(d * d, axis=-1, keepdims=True)
    y = d * lax.rsqrt(var + LN_EPS) * g_ref[...] + b_ref[...]
    o_ref[...] = y
    ob_ref[...] = y.astype(BF16)


def deepnorm_ln(x, mix, g, b, tm=128):
    t, d = x.shape
    tm = min(tm, t)
    row = pl.BlockSpec((tm, d), lambda i: (i, 0))
    vec = pl.BlockSpec((1, d), lambda i: (0, 0))
    return pl.pallas_call(
        _ln_kernel,
        grid=(t // tm,),
        in_specs=[row, row, vec, vec],
        out_specs=[row, row],
        out_shape=[jax.ShapeDtypeStruct((t, d), F32), jax.ShapeDtypeStruct((t, d), BF16)],
        compiler_params=_params(("parallel",)),
        name="deepnorm_ln",
    )(x, mix, g.reshape(1, d), b.reshape(1, d))


def _rms_kernel(x_ref, g_ref, o_ref):
    x = x_ref[...]
    y = x * lax.rsqrt(jnp.mean(x * x, axis=-1, keepdims=True) + RMS_EPS) * g_ref[...]
    o_ref[...] = y.astype(o_ref.dtype)


def rms_norm(x, g, tm=256):
    t, d = x.shape
    tm = min(tm, t)
    return pl.pallas_call(
        _rms_kernel,
        grid=(t // tm,),
        in_specs=[pl.BlockSpec((tm, d), lambda i: (i, 0)), pl.BlockSpec((1, d), lambda i: (0, 0))],
        out_specs=pl.BlockSpec((tm, d), lambda i: (i, 0)),
        out_shape=jax.ShapeDtypeStruct((t, d), BF16),
        compiler_params=_params(("parallel",)),
        name="rms_norm",
    )(x, g.reshape(1, d))


def _rope_kernel(x_ref, pos_ref, f_ref, o_ref):
    lane = lax.broadcasted_iota(I32, (1, LANES), 1)
    first = (lane % QK_ROPE) < (QK_ROPE // 2)
    ang = pos_ref[...].astype(F32) * f_ref[...]
    cos = jnp.cos(ang)
    sin = jnp.sin(ang)
    sin = jnp.where(first, -sin, sin)
    for c in range(x_ref.shape[1] // LANES):
        x = x_ref[:, c * LANES:(c + 1) * LANES]
        partner = jnp.where(first, pltpu.roll(x, LANES - QK_ROPE // 2, 1), pltpu.roll(x, QK_ROPE // 2, 1))
        o_ref[:, c * LANES:(c + 1) * LANES] = (x * cos + partner * sin).astype(o_ref.dtype)


def rope(x, pos_col, inv_freq_lanes, tm=256):
    t, w = x.shape
    tm = min(tm, t)
    return pl.pallas_call(
        _rope_kernel,
        grid=(t // tm,),
        in_specs=[pl.BlockSpec((tm, w), lambda i: (i, 0)),
                  pl.BlockSpec((tm, 1), lambda i: (i, 0)),
                  pl.BlockSpec((1, LANES), lambda i: (0, 0))],
        out_specs=pl.BlockSpec((tm, w), lambda i: (i, 0)),
        out_shape=jax.ShapeDtypeStruct((t, w), BF16),
        compiler_params=_params(("parallel",)),
        name="rope",
    )(x, pos_col, inv_freq_lanes)


def _sb_kernel(q_ref, k_ref, v_ref, o_ref, *, tile, group):
    i = pl.program_id(2)
    row = lax.broadcasted_iota(I32, (tile, tile), 0)
    col = lax.broadcasted_iota(I32, (tile, tile), 1)
    later = (row > col).astype(BF16)
    strict = col < row
    qs = [q_ref[:, h * HEAD_DIM:(h + 1) * HEAD_DIM] for h in range(group)]

    def step(j, state, diag):
        sl = pl.ds(pl.multiple_of(j * tile, tile), tile)
        heads = range(group)
        hs = [slice(h * HEAD_DIM, (h + 1) * HEAD_DIM) for h in heads]
        z = [lax.dot_general(qs[h], k_ref[sl, hs[h]], _NT, preferred_element_type=F32) for h in heads]
        u = [jnp.log2(1.0 + jnp.exp2(-jnp.abs(z[h]))) for h in heads]
        log_sig = [jnp.minimum(z[h], 0.0) - u[h] for h in heads]
        drop = [z[h] - log_sig[h] for h in heads]
        if diag:
            drop = [jnp.where(strict, d, 0.0) for d in drop]
        after = [jnp.dot(drop[h].astype(BF16), later, preferred_element_type=F32) for h in heads]
        w = [jnp.exp2(log_sig[h] - after[h] - state[h][0]) for h in heads]
        if diag:
            w = [jnp.where(strict, x, 0.0) for x in w]
        pv = [jnp.dot(w[h].astype(BF16), v_ref[sl, hs[h]], preferred_element_type=F32) for h in heads]
        return tuple((state[h][0] + (after[h][:, 0:1] + drop[h][:, 0:1]), state[h][1] + pv[h]) for h in heads)

    init = tuple((jnp.zeros((tile, 1), F32), jnp.zeros((tile, HEAD_DIM), F32)) for _ in range(group))
    state = step(i, init, True)
    state = lax.fori_loop(0, i, lambda n, st: step(i - 1 - n, st, False), state)
    for h in range(group):
        o_ref[:, h * HEAD_DIM:(h + 1) * HEAD_DIM] = state[h][1].astype(o_ref.dtype)


def sb_attention(qkv, batch, tile=256, group=4):
    t = qkv.shape[0]
    s = t // batch
    h = qkv.shape[1] // (3 * HEAD_DIM)
    tile, group = min(tile, s), min(group, h)
    nq = s // tile
    ng = h // group
    w = group * HEAD_DIM
    return pl.pallas_call(
        functools.partial(_sb_kernel, tile=tile, group=group),
        grid=(batch, ng, nq),
        in_specs=[pl.BlockSpec((tile, w), lambda b, hh, i: (b * nq + i, hh)),
                  pl.BlockSpec((s, w), lambda b, hh, i: (b, ng + hh)),
                  pl.BlockSpec((s, w), lambda b, hh, i: (b, 2 * ng + hh))],
        out_specs=pl.BlockSpec((tile, w), lambda b, hh, i: (b * nq + i, hh)),
        out_shape=jax.ShapeDtypeStruct((t, h * HEAD_DIM), BF16),
        compiler_params=_params(("parallel", "parallel", "arbitrary")),
        name="sb_attention",
    )(qkv, qkv, qkv)


def _t5_bucket(dist):
    n = jnp.maximum(dist, 0)
    max_exact = N_BUCKETS // 2
    scaled = jnp.log(jnp.maximum(n, 1).astype(F32) / max_exact) / math.log(MAX_DISTANCE / max_exact)
    large = jnp.minimum(max_exact + (scaled * (N_BUCKETS - max_exact)).astype(I32), N_BUCKETS - 1)
    return jnp.where(n < max_exact, n, large)


def _dsa_kernel(far_ref,
                q_ref, k_ref, v_ref, qi_ref, wk_q_ref, wk_k_ref, posq_ref, posk_ref, rbt_ref, rbfar_ref,
                o_ref,
                keys_ref, wb_ref, qall_ref, m_ref, acc_ref, s_ref,
                *, tq, tk, ts, chunk, rb, nk, topk, n_heads, hc, par):
    b = pl.program_id(0)
    i = pl.program_id(1)
    nq = pl.num_programs(1)
    n_causal = (i + 1) * tq
    n_chunks = (n_causal + chunk - 1) // chunk

    qi = qi_ref[...].astype(BF16)
    for hh in range(N_IDX_HEADS):
        wb_ref[hh] = jnp.broadcast_to(wk_q_ref[:, IDX_DIM + hh:IDX_DIM + hh + 1], (tq, ts))

    def to_key(sc):
        sc = jnp.where(sc == 0.0, 0.0, sc)
        bits = pltpu.bitcast(sc, I32)
        return bits ^ ((bits >> 31) & 0x7FFFFFFF)

    neg_key = to_key(jnp.full((tq, ts), NEG_INF, F32))
    t_idx_s = i * tq + lax.broadcasted_iota(I32, (tq, ts), 0)
    col_s = lax.broadcasted_iota(I32, (tq, ts), 1)

    def score_tile(j, _):
        sl = pl.ds(pl.multiple_of(j * ts, ts), ts)

        @pl.when(j * ts < n_causal)
        def _():
            kj = wk_k_ref[sl, 0:IDX_DIM].astype(BF16)
            sc = jnp.zeros((tq, ts), F32)
            for hh in range(N_IDX_HEADS):
                lg = lax.dot_general(qi[:, hh * IDX_DIM:(hh + 1) * IDX_DIM], kj, _NT,
                                     preferred_element_type=F32)
                sc = sc + wb_ref[hh] * jnp.maximum(lg, 0.0)
            sc = sc * (N_IDX_HEADS ** -0.5)
            sc = jnp.where(j * ts + col_s <= t_idx_s, sc, NEG_INF)
            keys_ref[:, sl] = to_key(sc)

        @pl.when(j * ts >= n_causal)
        def _():
            keys_ref[:, sl] = neg_key

        return 0

    lax.fori_loop(0, n_chunks * (chunk // ts), score_tile, 0)

    n_rb = tq // rb

    def count(pred, ref_val):
        ref_b = [jnp.broadcast_to(ref_val[r * rb:(r + 1) * rb, :], (rb, LANES)) for r in range(n_rb)]

        def body(c, accs):
            accs = list(accs)
            for u in range(chunk // LANES):
                sl = pl.ds(pl.multiple_of(c * chunk + u * LANES, LANES), LANES)
                for r in range(n_rb):
                    hit = pred(keys_ref[r * rb:(r + 1) * rb, sl], ref_b[r])
                    accs[r] = accs[r] + jnp.where(hit, 1, 0)
            return tuple(accs)

        accs = lax.fori_loop(0, n_chunks, body, tuple(jnp.zeros((rb, LANES), I32) for _ in range(n_rb)))
        return jnp.concatenate([jnp.sum(a, axis=1, keepdims=True) for a in accs], axis=0)

    def bisect(it, off):
        trial = off | jnp.left_shift(jnp.int32(1), 31 - it)
        cand = trial + jnp.int32(INT_MIN)
        return jnp.where(count(lambda kk, c: kk >= c, cand) >= topk, trial, off)

    thr = lax.fori_loop(0, 32, bisect, jnp.zeros((tq, 1), I32)) + jnp.int32(INT_MIN)
    need = (topk - count(lambda kk, c: kk > c, thr)).astype(F32)

    for hh in range(n_heads):
        qall_ref[hh * tq:(hh + 1) * tq, :] = q_ref[:, hh * HEAD_DIM:(hh + 1) * HEAD_DIM]
    m_ref[...] = jnp.full(m_ref.shape, NEG_INF, F32)
    acc_ref[...] = jnp.zeros(acc_ref.shape, F32)
    r2 = lax.broadcasted_iota(I32, (tk, tk), 0)
    c2 = lax.broadcasted_iota(I32, (tk, tk), 1)
    earlier = (r2 < c2).astype(BF16)
    ones_col = (lax.broadcasted_iota(I32, (tk, LANES), 1) == 0).astype(BF16)
    pos_q = posq_ref[...]
    n_hc = n_heads // hc
    rows_c = hc * tq
    n_tiles = (n_causal + tk - 1) // tk
    t_idx = i * tq + lax.broadcasted_iota(I32, (tq, tk), 0)
    col = lax.broadcasted_iota(I32, (tq, tk), 1)

    def attend_tile(j, eq_before):
        sl = pl.ds(pl.multiple_of(j * tk, tk), tk)
        kt = keys_ref[:, sl]
        eq = jnp.where(kt == thr, 1.0, 0.0)
        rank = eq_before + jnp.dot(eq.astype(BF16), earlier, preferred_element_type=F32)
        take = jnp.where(kt > thr, 1.0, jnp.where(rank < need, eq, 0.0))
        mask = jnp.where(j * tk + col <= t_idx, take, 0.0) > 0.5
        is_far = far_ref[(b * nq + i) * nk + j] == 1
        k_t = k_ref[sl, :]
        v_ext = jnp.concatenate([v_ref[sl, :], ones_col], axis=1)

        def near_bias(c):
            bucket = _t5_bucket(pos_q - posk_ref[0, :, sl])
            per_head = []
            for hh in range(c * hc, (c + 1) * hc):
                tbl = jnp.broadcast_to(rbt_ref[hh:hh + 1, :], (tq, LANES))
                halves = [jnp.take_along_axis(tbl, bucket[:, x * LANES:(x + 1) * LANES], axis=1,
                                              mode="promise_in_bounds") for x in range(tk // LANES)]
                per_head.append(jnp.concatenate(halves, axis=1))
            return jnp.stack(per_head, axis=0)

        for g in range(n_hc // par):
            chunks = range(par)
            rows = [slice((g * par + c) * rows_c, (g * par + c + 1) * rows_c) for c in chunks]
            local = [slice(c * rows_c, (c + 1) * rows_c) for c in chunks]
            for c in chunks:
                s_ref[local[c], :] = lax.dot_general(qall_ref[rows[c], :], k_t, _NT, preferred_element_type=F32)

            @pl.when(jnp.logical_not(is_far))
            def _():
                for c in chunks:
                    s_ref[local[c], :] += near_bias(g * par + c).reshape(rows_c, tk)

            s = [jnp.where(mask[None], s_ref[local[c], :].reshape(hc, tq, tk), NEG_INF).reshape(rows_c, tk)
                 for c in chunks]
            shift = [jnp.where(is_far, rbfar_ref[rows[c], :], 0.0) for c in chunks]
            m_old = [m_ref[rows[c], :] for c in chunks]
            m_new = [jnp.maximum(m_old[c], jnp.max(s[c], axis=1, keepdims=True) + shift[c]) for c in chunks]
            p = [jnp.exp2(s[c] - (m_new[c] - shift[c])).astype(BF16) for c in chunks]
            pv = [jnp.dot(p[c], v_ext, preferred_element_type=F32) for c in chunks]
            for c in chunks:
                acc_ref[rows[c], :] = jnp.exp2(m_old[c] - m_new[c]) * acc_ref[rows[c], :] + pv[c]
                m_ref[rows[c], :] = m_new[c]
        return eq_before + jnp.sum(eq, axis=1, keepdims=True)

    lax.fori_loop(0, n_tiles, attend_tile, jnp.zeros((tq, 1), F32))
    for hh in range(n_heads):
        rows = slice(hh * tq, (hh + 1) * tq)
        o_ref[:, hh * HEAD_DIM:(hh + 1) * HEAD_DIM] = (
            acc_ref[rows, 0:HEAD_DIM] / acc_ref[rows, HEAD_DIM:HEAD_DIM + 1]).astype(o_ref.dtype)


def dsa_attention(qkv, idx, positions, rel_bias, batch, tq=256, tk=512, ts=512, chunk=1024, rb=128, hc=4,
                  par=2):
    t = qkv.shape[0]
    s = t // batch
    n_heads = (qkv.shape[1] - 2 * HEAD_DIM) // HEAD_DIM
    tq, tk, ts, chunk, rb = min(tq, s), min(tk, s), min(ts, s), min(chunk, s), min(rb, tq)
    assert chunk % tk == 0 and chunk % ts == 0 and s % chunk == 0 and tq % rb == 0
    hc = min(hc, n_heads)
    par = min(par, n_heads // hc)
    nq, nk = s // tq, s // tk
    topk = min(TOPK_MAX, s // 4)
    qi_w = N_IDX_HEADS * IDX_DIM
    pos_lo = positions.reshape(batch, nq, tq).min(axis=-1)
    pos_hi = positions.reshape(batch, nk, tk).max(axis=-1)
    far = (pos_lo[:, :, None] - pos_hi[:, None, :] >= MAX_DISTANCE).astype(I32).reshape(-1)
    rb2 = rel_bias.astype(F32) * LOG2E
    rbt = jnp.pad(rb2.T, ((0, 0), (0, LANES - N_BUCKETS)))
    rbfar = jnp.repeat(rb2[N_BUCKETS - 1], tq).reshape(n_heads * tq, 1)
    kernel = functools.partial(_dsa_kernel, tq=tq, tk=tk, ts=ts, chunk=chunk, rb=rb, nk=nk, topk=topk,
                               n_heads=n_heads, hc=hc, par=par)
    once = pl.Buffered(1)
    grid_spec = pltpu.PrefetchScalarGridSpec(
        num_scalar_prefetch=1,
        grid=(batch, nq),
        in_specs=[
            pl.BlockSpec((tq, n_heads * HEAD_DIM), lambda b, i, *_: (b * nq + i, 0)),
            pl.BlockSpec((s, HEAD_DIM), lambda b, i, *_: (b, n_heads), pipeline_mode=once),
            pl.BlockSpec((s, HEAD_DIM), lambda b, i, *_: (b, n_heads + 1), pipeline_mode=once),
            pl.BlockSpec((tq, qi_w), lambda b, i, *_: (b * nq + i, 0)),
            pl.BlockSpec((tq, LANES), lambda b, i, *_: (b * nq + i, qi_w // LANES)),
            pl.BlockSpec((s, LANES), lambda b, i, *_: (b, qi_w // LANES), pipeline_mode=once),
            pl.BlockSpec((tq, 1), lambda b, i, *_: (b * nq + i, 0)),
            pl.BlockSpec((1, 1, s), lambda b, i, *_: (b, 0, 0), pipeline_mode=once),
            pl.BlockSpec((n_heads, LANES), lambda b, i, *_: (0, 0), pipeline_mode=once),
            pl.BlockSpec((n_heads * tq, 1), lambda b, i, *_: (0, 0), pipeline_mode=once),
        ],
        out_specs=pl.BlockSpec((tq, n_heads * HEAD_DIM), lambda b, i, *_: (b * nq + i, 0)),
        scratch_shapes=[
            pltpu.VMEM((tq, s), I32),
            pltpu.VMEM((N_IDX_HEADS, tq, ts), F32),
            pltpu.VMEM((n_heads * tq, HEAD_DIM), BF16),
            pltpu.VMEM((n_heads * tq, 1), F32),
            pltpu.VMEM((n_heads * tq, 2 * HEAD_DIM), F32),
            pltpu.VMEM((par * hc * tq, tk), F32),
        ],
    )
    return pl.pallas_call(
        kernel,
        grid_spec=grid_spec,
        out_shape=jax.ShapeDtypeStruct((t, n_heads * HEAD_DIM), BF16),
        compiler_params=_params(("parallel", "arbitrary")),
        name="dsa_attention",
    )(far, qkv, qkv, qkv, idx, idx, idx, positions.reshape(t, 1), positions.reshape(batch, 1, s),
      rbt, rbfar)


def _mla_kernel(qn_ref, qr_ref, kn_ref, kr_ref, v_ref, o_ref, kcat_ref, vext_ref, *, tq, tkb, group):
    i = pl.program_id(2)
    wide = 2 * LANES

    @pl.when(i == 0)
    def _():
        ones_col = (lax.broadcasted_iota(I32, (kr_ref.shape[0], LANES), 1) == 0).astype(BF16)
        for h in range(group):
            kcat_ref[:, h * wide:h * wide + LANES] = kn_ref[:, h * LANES:(h + 1) * LANES]
            kcat_ref[:, h * wide + LANES:(h + 1) * wide] = kr_ref[...]
            vext_ref[:, h * wide:h * wide + LANES] = v_ref[:, h * LANES:(h + 1) * LANES]
            vext_ref[:, h * wide + LANES:(h + 1) * wide] = ones_col

    row = lax.broadcasted_iota(I32, (tq, tkb), 0)
    col = lax.broadcasted_iota(I32, (tq, tkb), 1)
    qs = [jnp.concatenate([qn_ref[:, h * LANES:(h + 1) * LANES], qr_ref[:, h * LANES:(h + 1) * LANES]], axis=1)
          for h in range(group)]
    n_full = (i * tq) // tkb

    def chunk(c, state, masked):
        sl = pl.ds(pl.multiple_of(c * tkb, tkb), tkb)
        heads = range(group)
        hs = [slice(h * wide, (h + 1) * wide) for h in heads]
        s = [lax.dot_general(qs[h], kcat_ref[sl, hs[h]], _NT, preferred_element_type=F32) for h in heads]
        if masked:
            s = [jnp.where(c * tkb + col <= i * tq + row, x, NEG_INF) for x in s]
        m_new = [jnp.maximum(state[h][0], jnp.max(s[h], axis=1, keepdims=True)) for h in heads]
        p = [jnp.exp2(s[h] - m_new[h]).astype(BF16) for h in heads]
        pv = [jnp.dot(p[h], vext_ref[sl, hs[h]], preferred_element_type=F32) for h in heads]
        return tuple((m_new[h], jnp.exp2(state[h][0] - m_new[h]) * state[h][1] + pv[h]) for h in heads)

    init = tuple((jnp.full((tq, 1), NEG_INF, F32), jnp.zeros((tq, 2 * LANES), F32)) for _ in range(group))
    state = lax.fori_loop(0, n_full, lambda c, st: chunk(c, st, False), init)
    state = chunk(n_full, state, True)
    for h in range(group):
        acc = state[h][1]
        o_ref[:, h * LANES:(h + 1) * LANES] = (acc[:, 0:V_DIM] / acc[:, V_DIM:V_DIM + 1]).astype(o_ref.dtype)


def mla_attention(qn, qr, kn, kr, v, batch, tq=512, tkb=1024, group=2):
    t = qn.shape[0]
    s = t // batch
    h = qn.shape[1] // QK_NOPE
    tq, tkb, group = min(tq, s), min(tkb, s), min(group, h)
    nq = s // tq
    w = group * LANES
    qspec = pl.BlockSpec((tq, w), lambda b, hh, i: (b * nq + i, hh))
    kspec = pl.BlockSpec((s, w), lambda b, hh, i: (b, hh))
    return pl.pallas_call(
        functools.partial(_mla_kernel, tq=tq, tkb=tkb, group=group),
        grid=(batch, h // group, nq),
        in_specs=[qspec, qspec, kspec, pl.BlockSpec((s, LANES), lambda b, hh, i: (b, 0)), kspec],
        out_specs=qspec,
        out_shape=jax.ShapeDtypeStruct((t, h * V_DIM), BF16),
        scratch_shapes=[pltpu.VMEM((s, 2 * w), BF16),
                        pltpu.VMEM((s, 2 * w), BF16)],
        compiler_params=_params(("parallel", "parallel", "arbitrary")),
        name="mla_attention",
    )(qn, qr, kn, kr, v)


def _first_max(vals):
    best = vals[0]
    for v in vals[1:]:
        best = jnp.maximum(best, v)
    idx = jnp.full(best.shape, len(vals) - 1, I32)
    for k in range(len(vals) - 2, -1, -1):
        idx = jnp.where(vals[k] == best, k, idx)
    return best, idx


def _router_kernel(x_ref, w_ref, b_ref, e_ref, wt_ref):
    x = x_ref[...]
    x_hi = x.astype(BF16)
    x_lo = (x - x_hi.astype(F32)).astype(BF16)
    wide = (jnp.dot(x_hi, w_ref[0], preferred_element_type=F32)
            + jnp.dot(x_hi, w_ref[1], preferred_element_type=F32)
            + jnp.dot(x_lo, w_ref[0], preferred_element_type=F32))
    logits = wide.T[0:N_EXPERTS, :]
    aff = jax.nn.sigmoid(logits)
    biased = aff + b_ref[...]
    a_rows = [aff[e:e + 1, :] for e in range(N_EXPERTS)]
    b_rows = [biased[e:e + 1, :] for e in range(N_EXPERTS)]
    g_score, g_e1, g_e2 = [], [], []
    for g in range(N_GROUPS):
        vals = b_rows[g * EXPERTS_PER_GROUP:(g + 1) * EXPERTS_PER_GROUP]
        top1, i1 = _first_max(vals)
        rest = [jnp.where(i1 == k, -jnp.inf, vals[k]) for k in range(EXPERTS_PER_GROUP)]
        top2, i2 = _first_max(rest)
        g_score.append(top1 + top2)
        g_e1.append(i1 + g * EXPERTS_PER_GROUP)
        g_e2.append(i2 + g * EXPERTS_PER_GROUP)
    _, grp = _first_max(g_score)
    e1, e2 = g_e1[N_GROUPS - 1], g_e2[N_GROUPS - 1]
    for g in range(N_GROUPS - 2, -1, -1):
        e1 = jnp.where(grp == g, g_e1[g], e1)
        e2 = jnp.where(grp == g, g_e2[g], e2)
    w1 = jnp.zeros_like(a_rows[0])
    w2 = jnp.zeros_like(a_rows[0])
    for e in range(N_EXPERTS):
        w1 = jnp.where(e1 == e, a_rows[e], w1)
        w2 = jnp.where(e2 == e, a_rows[e], w2)
    tot = w1 + w2
    e_ref[0:1, :] = e1
    e_ref[1:2, :] = e2
    wt_ref[0:1, :] = w1 / tot
    wt_ref[1:2, :] = w2 / tot


def router(x, w_router, router_bias, tm=512):
    t, d = x.shape
    tm = min(tm, t)
    w = jnp.pad(w_router.astype(F32), ((0, 0), (0, LANES - N_EXPERTS)))
    w_hi = w.astype(BF16)
    w_split = jnp.stack([w_hi, (w - w_hi.astype(F32)).astype(BF16)])
    return pl.pallas_call(
        _router_kernel,
        grid=(t // tm,),
        in_specs=[pl.BlockSpec((tm, d), lambda i: (i, 0)),
                  pl.BlockSpec((2, d, LANES), lambda i: (0, 0, 0)),
                  pl.BlockSpec((N_EXPERTS, 1), lambda i: (0, 0))],
        out_specs=[pl.BlockSpec((TOP_K_EXPERTS, tm), lambda i: (0, i)),
                   pl.BlockSpec((TOP_K_EXPERTS, tm), lambda i: (0, i))],
        out_shape=[jax.ShapeDtypeStruct((TOP_K_EXPERTS, t), I32),
                   jax.ShapeDtypeStruct((TOP_K_EXPERTS, t), F32)],
        compiler_params=_params(("parallel",)),
        name="router",
    )(x, w_split, router_bias.reshape(N_EXPERTS, 1))


def dispatch_plan(experts, tm):
    k, t = experts.shape
    n_slots = k * t
    n_rows = n_slots + N_EXPERTS * tm
    n_tiles = n_rows // tm
    flat = experts.reshape(n_slots)
    order = jnp.argsort(flat, stable=True).astype(I32)
    inverse = jnp.argsort(order).astype(I32)
    ids = jnp.arange(N_EXPERTS, dtype=I32)
    counts = jnp.sum((flat[:, None] == ids[None, :]).astype(I32), axis=0)
    padded = (counts + tm - 1) // tm * tm
    pad_end = jnp.cumsum(padded)
    pad_start = pad_end - padded
    start = jnp.cumsum(counts) - counts
    dest = pad_start[flat] + inverse - start[flat]
    tile_start = jnp.arange(n_tiles, dtype=I32) * tm
    tile_expert = jnp.minimum(jnp.sum((tile_start[:, None] >= pad_end[None, :]).astype(I32), axis=1),
                              N_EXPERTS - 1)
    row = jnp.arange(n_rows, dtype=I32)
    row_expert = jnp.repeat(tile_expert, tm)
    rank = row - pad_start[row_expert]
    sorted_pos = jnp.clip(start[row_expert] + rank, 0, n_slots - 1)
    row_token = jnp.where(rank < counts[row_expert], order[sorted_pos] % t, 0)
    tile_active = (tile_start < pad_end[-1]).astype(I32)
    return row_token.reshape(n_tiles, 1, tm), dest.reshape(k, t), tile_expert, tile_active


def _ffn_kernel(exp_ref, act_ref, idx_ref, idx_next_ref, x_hbm, wg_ref, wu_ref, wd_ref, o_ref,
                land, xs, sem, *, tm):
    t = pl.program_id(0)
    last = pl.num_programs(0) - 1

    def row_copy(ids, r):
        return pltpu.make_async_copy(x_hbm.at[pl.ds(ids[0, 0, r], 1), :], land.at[pl.ds(r, 1), :], sem)

    def start_tile(ids):
        def body(r, c):
            row_copy(ids, r).start()
            return c
        lax.fori_loop(0, tm, body, 0, unroll=8)

    def wait_tile(ids):
        def body(r, c):
            row_copy(ids, r).wait()
            return c
        lax.fori_loop(0, tm, body, 0, unroll=8)

    @pl.when(jnp.logical_and(t == 0, act_ref[0] == 1))
    def _():
        start_tile(idx_ref)

    @pl.when(act_ref[t] == 1)
    def _():
        wait_tile(idx_ref)
        xs[...] = land[...].astype(BF16)

    def ffn():
        x = xs[...]
        h = jax.nn.silu(jnp.dot(x, wg_ref[0], preferred_element_type=F32)) * jnp.dot(
            x, wu_ref[0], preferred_element_type=F32)
        o_ref[...] = jnp.dot(h.astype(BF16), wd_ref[0], preferred_element_type=F32)

    fetch_next = jnp.logical_and(t < last, act_ref[jnp.minimum(t + 1, last)] == 1)

    @pl.when(jnp.logical_and(act_ref[t] == 1, fetch_next))
    def _():
        for r in range(tm):
            row_copy(idx_next_ref, r).start()
        ffn()

    @pl.when(jnp.logical_and(act_ref[t] == 1, jnp.logical_not(fetch_next)))
    def _():
        ffn()

    @pl.when(act_ref[t] == 0)
    def _():
        o_ref[...] = jnp.zeros_like(o_ref)


def expert_ffn(x, row_token, tile_expert, tile_active, w_gate, w_up, w_down):
    n_tiles, _, tm = row_token.shape
    d = x.shape[1]
    f = w_gate.shape[2]
    ids = lambda shift: pl.BlockSpec(
        (1, 1, tm), lambda t, ex, ac: (jnp.minimum(t + shift, n_tiles - 1), 0, 0), memory_space=pltpu.SMEM)
    grid_spec = pltpu.PrefetchScalarGridSpec(
        num_scalar_prefetch=2,
        grid=(n_tiles,),
        in_specs=[ids(0), ids(1),
                  pl.BlockSpec(memory_space=pl.ANY),
                  pl.BlockSpec((1, d, f), lambda t, ex, ac: (ex[t], 0, 0)),
                  pl.BlockSpec((1, d, f), lambda t, ex, ac: (ex[t], 0, 0)),
                  pl.BlockSpec((1, f, d), lambda t, ex, ac: (ex[t], 0, 0))],
        out_specs=pl.BlockSpec((tm, d), lambda t, ex, ac: (t, 0)),
        scratch_shapes=[pltpu.VMEM((tm, d), F32),
                        pltpu.VMEM((tm, d), BF16),
                        pltpu.SemaphoreType.DMA(())],
    )
    return pl.pallas_call(
        functools.partial(_ffn_kernel, tm=tm),
        grid_spec=grid_spec,
        out_shape=jax.ShapeDtypeStruct((n_tiles * tm, d), F32),
        compiler_params=_params(("arbitrary",)),
        name="expert_ffn",
    )(tile_expert, tile_active, row_token, row_token, x, w_gate, w_up, w_down)


def _combine_ln_kernel(dest_ref, dest_next_ref, x_ref, w_ref, g_ref, b_ref, y_hbm, o_ref, ob_ref, buf, sem,
                       *, tm):
    i = pl.program_id(0)
    last = pl.num_programs(0) - 1
    slot = i % 2

    def row_copy(ids, k, r, s):
        return pltpu.make_async_copy(y_hbm.at[pl.ds(ids[0, k, r], 1), :], buf.at[s, k, pl.ds(r, 1), :],
                                     sem.at[s])

    def start_tile(ids, s):
        def body(r, c):
            for k in range(TOP_K_EXPERTS):
                row_copy(ids, k, r, s).start()
            return c
        lax.fori_loop(0, tm, body, 0, unroll=8)

    @pl.when(i == 0)
    def _():
        start_tile(dest_ref, 0)

    def wait(r, c):
        for k in range(TOP_K_EXPERTS):
            row_copy(dest_ref, k, r, slot).wait()
        return c

    lax.fori_loop(0, tm, wait, 0, unroll=8)

    def combine():
        h = DEEPNORM_ALPHA * x_ref[...] + (w_ref[:, 0:1] * buf[slot, 0] + w_ref[:, 1:2] * buf[slot, 1])
        mu = jnp.mean(h, axis=-1, keepdims=True)
        dlt = h - mu
        var = jnp.mean(dlt * dlt, axis=-1, keepdims=True)
        y = dlt * lax.rsqrt(var + LN_EPS) * g_ref[...] + b_ref[...]
        o_ref[...] = y
        ob_ref[...] = y.astype(BF16)

    @pl.when(i < last)
    def _():
        for r in range(tm):
            for k in range(TOP_K_EXPERTS):
                row_copy(dest_next_ref, k, r, 1 - slot).start()
        combine()

    @pl.when(i == last)
    def _():
        combine()


def combine_ln(x, ys, dest, weights, g, b, tm=256):
    t, d = x.shape
    tm = min(tm, t)
    k = dest.shape[0]
    n = t // tm
    dest_tiles = dest.reshape(k, n, tm).transpose(1, 0, 2)
    row = pl.BlockSpec((tm, d), lambda i: (i, 0))
    vec = pl.BlockSpec((1, d), lambda i: (0, 0))
    ids = lambda shift: pl.BlockSpec((1, k, tm), lambda i: (jnp.minimum(i + shift, n - 1), 0, 0),
                                     memory_space=pltpu.SMEM)
    return pl.pallas_call(
        functools.partial(_combine_ln_kernel, tm=tm),
        grid=(n,),
        in_specs=[ids(0), ids(1),
                  row,
                  pl.BlockSpec((tm, k), lambda i: (i, 0)),
                  vec, vec,
                  pl.BlockSpec(memory_space=pl.ANY)],
        out_specs=[row, row],
        out_shape=[jax.ShapeDtypeStruct((t, d), F32), jax.ShapeDtypeStruct((t, d), BF16)],
        scratch_shapes=[pltpu.VMEM((2, k, tm, d), F32), pltpu.SemaphoreType.DMA((2,))],
        compiler_params=_params(("arbitrary",)),
        name="combine_ln",
    )(dest_tiles, dest_tiles, x, weights.T, g.reshape(1, d), b.reshape(1, d), ys)


def even_mixer(xb, positions, w_in_all, w_out_all, layer, rel_bias, batch):
    q_scale = HEAD_DIM ** -0.5 * LOG2E
    o2, o3, o5 = 3 * SB_W, 3 * SB_W + DSA_W, 3 * SB_W + DSA_W + 2 * HEAD_DIM
    o6 = o5 + N_IDX_HEADS * IDX_DIM
    o7 = o6 + N_IDX_HEADS
    scale_sb = jnp.where(jnp.arange(o2) < SB_W, q_scale, 1.0)
    scale_ds = jnp.where(jnp.arange(o5 - o2) < DSA_W, q_scale, 1.0)
    qkv_sb = matmul_w32(xb, w_in_all, layer, 0, o2, scale_sb, BF16, 1024, 512)
    qkv_ds = matmul_w32(xb, w_in_all, layer, o2, o5 - o2, scale_ds, BF16, 1024, 256)
    w_in = w_in_all[layer, :, o5:]
    pad = jnp.zeros((w_in.shape[0], LANES - IDX_DIM - N_IDX_HEADS), w_in.dtype)
    w_ix = jnp.concatenate([w_in[:, :o6 - o5] * IDX_DIM ** -0.5, w_in[:, o7 - o5:], w_in[:, o6 - o5:o7 - o5], pad],
                           axis=1).astype(BF16)
    idx = matmul(xb, w_ix, F32, 1024, w_ix.shape[1])
    o_sb = sb_attention(qkv_sb, batch)
    o_ds = dsa_attention(qkv_ds, idx, positions, rel_bias, batch)
    return matmul2(o_sb, o_ds, w_out_all, layer, F32, 1024, 512)


def mla_mixer(xb, positions, w_down_all, layer, g_q, g_kv, w_uq, w_ukv, w_o_all, batch):
    t = xb.shape[0]
    scale = (QK_NOPE + QK_ROPE) ** -0.5 * LOG2E
    w_kr = w_down_all[layer, :, Q_LORA + KV_LORA:].astype(BF16)
    kr_pad = jnp.zeros((w_kr.shape[0], LANES - QK_ROPE), BF16)
    c_q = matmul_w32(xb, w_down_all, layer, 0, Q_LORA, jnp.ones((Q_LORA,)), F32, 1024, 512)
    c_kv = matmul_w32(xb, w_down_all, layer, Q_LORA, KV_LORA, jnp.ones((KV_LORA,)), F32, 1024, 512)
    k_rope = matmul(xb, jnp.concatenate([w_kr, kr_pad], axis=1), F32, 1024, LANES)
    c_q = rms_norm(c_q, g_q)
    c_kv = rms_norm(c_kv, g_kv)
    w_uq = (w_uq * scale).reshape(Q_LORA, N_HEADS_MLA, QK_NOPE + QK_ROPE)
    w_qn = w_uq[:, :, :QK_NOPE].reshape(Q_LORA, N_HEADS_MLA * QK_NOPE).astype(BF16)
    w_qr = jnp.pad(w_uq[:, :, QK_NOPE:], ((0, 0), (0, 0), (0, LANES - QK_ROPE)))
    w_qr = w_qr.reshape(Q_LORA, N_HEADS_MLA * LANES).astype(BF16)
    w_ukv = w_ukv.reshape(KV_LORA, N_HEADS_MLA, QK_NOPE + V_DIM)
    w_kn = w_ukv[:, :, :QK_NOPE].reshape(KV_LORA, N_HEADS_MLA * QK_NOPE).astype(BF16)
    w_v = w_ukv[:, :, QK_NOPE:].reshape(KV_LORA, N_HEADS_MLA * V_DIM).astype(BF16)
    qn = matmul(c_q, w_qn, BF16, 1024, 1024)
    qr = matmul(c_q, w_qr, F32, 1024, 1024)
    kn = matmul(c_kv, w_kn, BF16, 1024, 1024)
    v = matmul(c_kv, w_v, BF16, 1024, 1024)
    half = QK_ROPE // 2
    inv_freq = ROPE_THETA ** (-jnp.arange(half, dtype=F32) / half)
    inv_freq = jnp.tile(inv_freq, LANES // half).reshape(1, LANES)
    pos_col = positions.reshape(t, 1)
    qr = rope(qr, pos_col, inv_freq)
    kr = rope(k_rope, pos_col, inv_freq)
    o = mla_attention(qn, qr, kn, kr, v, batch)
    d_out = w_o_all.shape[2]
    return matmul_w32(o, w_o_all, layer, 0, d_out, jnp.ones((d_out,)), F32, 1024, 512)


def moe_ln(x, w_router, router_bias, w_gate, w_up, w_down, g, b, tm=256):
    experts, weights = router(x, w_router, router_bias)
    row_token, dest, tile_expert, tile_active = dispatch_plan(experts, tm)
    ys = expert_ffn(x, row_token, tile_expert, tile_active, w_gate.astype(BF16), w_up.astype(BF16),
                    w_down.astype(BF16))
    return combine_ln(x, ys, dest, weights, g, b)


def kernel(x, positions, rel_bias, even_w_in, even_w_out, mla_w_down, mla_g_q, mla_g_kv, mla_w_uq,
           mla_w_ukv, mla_w_o, w_router, router_bias, exp_w_gate, exp_w_up, exp_w_down, ln_g, ln_b):
    batch, seq, d = x.shape
    x = x.reshape(batch * seq, d)
    xb = x.astype(BF16)
    for layer in range(DEPTH):
        i = layer // 2
        if layer % 2 == 0:
            mix = even_mixer(xb, positions, even_w_in, even_w_out, i, rel_bias, batch)
        else:
            mix = mla_mixer(xb, positions, mla_w_down, i, mla_g_q[i], mla_g_kv[i], mla_w_uq[i],
                            mla_w_ukv[i], mla_w_o, batch)
        x, xb = deepnorm_ln(x, mix, ln_g[layer, 0], ln_b[layer, 0])
        x, xb = moe_ln(x, w_router, router_bias, exp_w_gate[layer], exp_w_up[layer], exp_w_down[layer],
                       ln_g[layer, 1], ln_b[layer, 1])
    return x.reshape(batch, seq, d)
```
